```python
import math
import jax, jax.numpy as jnp
from jax import lax
import numpy as np

D_MODEL = 2048
BATCH = 1
SEQ = 8192
DEPTH = 2

GRID_W = 64
CTX_LEN = 256
N_MIXERS = 2
RMS_EPS = 1e-6

S5_WIDTH = D_MODEL
S5_GROUP = 16
S5_GROUPS = S5_WIDTH // S5_GROUP
S5_STATE = 64
S5_MIN_DT = 1e-3
S5_MAX_DT = 1e-1
S5_EIG_CLIP = -1e-4

HY_WIDTH = D_MODEL
HY_ORDER = 2
HY_SHORT = 3
HY_FILTER_WIDTH = 64
HY_POS_DIM = 33
HY_BANDS = (HY_POS_DIM - 1) // 2
HY_DECAY_TARGET = 1e-2
HY_FAST_DECAY_PCT = 0.3
HY_SLOW_DECAY_PCT = 1.5
HY_MAX_DECAY = math.log(HY_DECAY_TARGET) / HY_FAST_DECAY_PCT
HY_MIN_DECAY = math.log(HY_DECAY_TARGET) / HY_SLOW_DECAY_PCT

N_EXPERTS = 32
TOP_K = 4
D_EXPERT = D_MODEL
SWIGLU_ALPHA = 1.702
SWIGLU_LIMIT = 7.0
MOE_BLOCK = 256

kernel_name = "hybrid_s5_hyena_moe_prefix_dit"


def rms_norm(x, g):
    xf = x.astype(jnp.float32)
    y = xf * lax.rsqrt(jnp.mean(xf * xf, axis=-1, keepdims=True) + RMS_EPS)
    return (y * g.astype(jnp.float32)).astype(x.dtype)


def modulate(h, shift, scale):
    return h * (1 + scale) + shift


def s5_discretize(a_re, a_im, log_dt, b_re, b_im, c_re, c_im):
    f32 = jnp.float32
    lam = lax.complex(jnp.minimum(a_re.astype(f32), S5_EIG_CLIP), a_im.astype(f32))
    dt = jnp.exp(log_dt.astype(f32))[..., None]
    lam_bar = jnp.exp(lam * dt)
    b_bar = ((lam_bar - 1) / lam)[..., None] * lax.complex(b_re.astype(f32), b_im.astype(f32))
    c_mat = lax.complex(c_re.astype(f32), c_im.astype(f32))
    return lam_bar, b_bar, c_mat


def _diag_combine(e1, e2):
    a1, b1 = e1
    a2, b2 = e2
    return a1 * a2, a2 * b1 + b2


def s5_bidirectional(u, lam_bar, b_bar, c_mat, d, h0, readout):
    bsz, n, _ = u.shape
    uf = u.astype(jnp.float32)
    ug = uf.reshape(bsz, n, S5_GROUPS, S5_GROUP)
    y = None
    finals = []
    for k in range(2):
        reverse = k == 1
        bu = lax.complex(jnp.einsum('blgc,gpc->blgp', ug, b_bar[k].real),
                         jnp.einsum('blgc,gpc->blgp', ug, b_bar[k].imag))
        if h0 is not None:
            edge = n - 1 if reverse else 0
            bu = bu.at[:, edge].add(lam_bar[k] * h0[k])
        a = jnp.broadcast_to(lam_bar[k], bu.shape)
        _, states = lax.associative_scan(_diag_combine, (a, bu), axis=1, reverse=reverse)
        finals.append(states[:, 0] if reverse else states[:, -1])
        if readout:
            yk = jnp.einsum('blgp,gcp->blgc', states, c_mat[k]).real
            y = yk if y is None else y + yk
    if readout:
        y = y.reshape(bsz, n, S5_WIDTH) + d.astype(jnp.float32) * uf
    return y, finals


def s5_glu(y, w_glu):
    z = jax.nn.gelu(y, approximate=False).astype(w_glu.dtype)
    a, g = jnp.split(jnp.dot(z, w_glu), 2, axis=-1)
    return a * jax.nn.sigmoid(g)


def short_conv(u, w, b):
    n = u.shape[1]
    pad = HY_SHORT // 2
    up = jnp.pad(u, ((0, 0), (pad, pad), (0, 0)))
    out = b
    for j in range(HY_SHORT):
        out = out + up[:, j:j + n] * w[j]
    return out


def hyena_filter_fft(n, f_w1, f_b1, f_w2, f_b2, f_w3, f_b3, f_freq, f_wout):
    f32 = jnp.float32
    t = jnp.linspace(0.0, 1.0, n, dtype=f32)[:, None]
    w = (2.0 * math.pi / n) * jnp.arange(n, dtype=f32)[:, None]
    bands = jnp.linspace(1e-4, HY_BANDS - 1, HY_BANDS, dtype=f32)[None, :]
    feats = jnp.concatenate([t, jnp.cos(bands * w), -jnp.sin(bands * w)], axis=-1)
    freq = f_freq.astype(f32)
    a = jnp.sin(freq * (feats @ f_w1.astype(f32) + f_b1.astype(f32)))
    a = jnp.sin(freq * (a @ f_w2.astype(f32) + f_b2.astype(f32)))
    a = jnp.sin(freq * (a @ f_w3.astype(f32) + f_b3.astype(f32)))
    k = (a @ f_wout.astype(f32)).reshape(n, 2, HY_ORDER, HY_WIDTH)
    decay = jnp.abs(jnp.linspace(HY_MIN_DECAY, HY_MAX_DECAY, HY_WIDTH, dtype=f32))
    k = k * jnp.exp(-t * decay)[:, None, None, :]
    k_full = jnp.concatenate([k[:, 0], jnp.zeros((1, HY_ORDER, HY_WIDTH), f32), k[:0:-1, 1]], axis=0)
    return jnp.fft.rfft(k_full, axis=0)


def hyena(h, w_in, conv_w, conv_b, f_w1, f_b1, f_w2, f_b2, f_w3, f_b3, f_freq, f_wout, f_bias, w_out):
    n = h.shape[1]
    u = short_conv(jnp.dot(h, w_in), conv_w, conv_b).astype(jnp.float32)
    x1, x2, v = jnp.split(u, 3, axis=-1)
    k_fft = hyena_filter_fft(n, f_w1, f_b1, f_w2, f_b2, f_w3, f_b3, f_freq, f_wout)
    z = v
    for o, gate in enumerate((x1, x2)):
        zf = jnp.fft.rfft(z, n=2 * n, axis=1)
        conv = jnp.fft.irfft(zf * k_fft[:, o], n=2 * n, axis=1)[:, :n]
        z = gate * (conv + z * f_bias[o].astype(jnp.float32))
    return jnp.dot(z.astype(h.dtype), w_out)


def moe_ffn(h, w_r, b_r, w1, b1, w2, b2):
    bsz, n, d = h.shape
    t = bsz * n
    hf = h.reshape(t, d)
    logits = (jnp.dot(hf, w_r) + b_r).astype(jnp.float32)
    top_v, top_i = lax.top_k(logits, TOP_K)
    gates = jax.nn.softmax(top_v, axis=-1)
    a = t * TOP_K
    e_flat = top_i.reshape(a)
    tok_flat = jnp.repeat(jnp.arange(t, dtype=jnp.int32), TOP_K)
    order = jnp.argsort(e_flat)
    e_s = e_flat[order]
    tok_s = tok_flat[order]
    g_s = gates.reshape(a)[order]
    counts = jnp.bincount(e_flat, length=N_EXPERTS)
    padded = (counts + MOE_BLOCK - 1) // MOE_BLOCK * MOE_BLOCK
    start = jnp.cumsum(counts) - counts
    pend = jnp.cumsum(padded)
    pstart = pend - padded
    dest = pstart[e_s] + jnp.arange(a, dtype=jnp.int32) - start[e_s]
    n_blocks = (a + N_EXPERTS * (MOE_BLOCK - 1) + MOE_BLOCK - 1) // MOE_BLOCK
    rows = n_blocks * MOE_BLOCK
    row_tok = jnp.full((rows,), t, jnp.int32).at[dest].set(tok_s)
    row_gate = jnp.zeros((rows,), jnp.float32).at[dest].set(g_s)
    blk_exp = jnp.minimum(jnp.searchsorted(pend, jnp.arange(n_blocks, dtype=jnp.int32) * MOE_BLOCK,
                                           side='right'), N_EXPERTS - 1)
    hpad = jnp.concatenate([hf, jnp.zeros((1, d), hf.dtype)], axis=0)

    def expert_block(args):
        tok_b, e = args
        xb = hpad[tok_b]
        hid = jnp.dot(xb, w1[e]) + b1[e]
        glu, lin = jnp.split(hid, 2, axis=-1)
        glu = jnp.minimum(glu, SWIGLU_LIMIT)
        lin = jnp.clip(lin, -SWIGLU_LIMIT, SWIGLU_LIMIT)
        act = glu * jax.nn.sigmoid(SWIGLU_ALPHA * glu) * (lin + 1)
        return jnp.dot(act, w2[e]) + b2[e]

    out = lax.map(expert_block, (row_tok.reshape(n_blocks, MOE_BLOCK), blk_exp))
    out = out.reshape(rows, d) * row_gate[:, None].astype(out.dtype)
    y = jax.ops.segment_sum(out, row_tok, num_segments=t + 1)[:t]
    return y.reshape(bsz, n, d)


def setup_inputs(seed: int = 0) -> dict:
    f32 = jnp.float32
    key = jax.random.key(seed)
    ks = iter(list(jax.random.split(key, 40)))

    def nrm(shape, std):
        return jax.random.normal(next(ks), shape, f32) * std

    n_a = len(range(0, DEPTH, N_MIXERS))
    n_b = len(range(1, DEPTH, N_MIXERS))
    D, E, H, F = D_MODEL, S5_WIDTH, HY_WIDTH, D_EXPERT
    G, P, Q, W = S5_GROUPS, S5_STATE, S5_GROUP, HY_FILTER_WIDTH
    return {
        'x': nrm((BATCH, SEQ, D), 1.0),
        'c': nrm((BATCH, D), 1.0),
        'ctx': nrm((BATCH, CTX_LEN, D), 1.0),
        'c_ctx': nrm((D,), 1.0),
        'ada_w': nrm((DEPTH, D, 6 * D), 0.5 * D ** -0.5),
        'ada_b': nrm((DEPTH, 6 * D), 0.01),
        'norm_g': 1.0 + nrm((DEPTH, 4, D), 0.05),
        's5_w_in': nrm((n_a, D, E), D ** -0.5),
        's5_a_re': -0.5 + nrm((n_a, 2, G, P), 0.01),
        's5_a_im': math.pi * jnp.arange(P, dtype=f32) + nrm((n_a, 2, G, P), 0.01),
        's5_log_dt': jax.random.uniform(next(ks), (n_a, 2, G), f32, math.log(S5_MIN_DT), math.log(S5_MAX_DT)),
        's5_b_re': nrm((n_a, 2, G, P, Q), (2 * Q) ** -0.5),
        's5_b_im': nrm((n_a, 2, G, P, Q), (2 * Q) ** -0.5),
        's5_c_re': nrm((n_a, 2, G, Q, P), (2 * P) ** -0.5),
        's5_c_im': nrm((n_a, 2, G, Q, P), (2 * P) ** -0.5),
        's5_d': nrm((n_a, E), 1.0),
        's5_w_glu': nrm((n_a, E, 2 * D), E ** -0.5),
        'hy_w_in': nrm((n_b, D, 3 * H), D ** -0.5),
        'hy_conv_w': nrm((n_b, HY_SHORT, 3 * H), HY_SHORT ** -0.5),
        'hy_conv_b': nrm((n_b, 3 * H), 0.01),
        'hy_f_w1': nrm((n_b, HY_POS_DIM, W), HY_POS_DIM ** -0.5),
        'hy_f_b1': nrm((n_b, W), 0.1),
        'hy_f_w2': nrm((n_b, W, W), W ** -0.5),
        'hy_f_b2': nrm((n_b, W), 0.1),
        'hy_f_w3': nrm((n_b, W, W), W ** -0.5),
        'hy_f_b3': nrm((n_b, W), 0.1),
        'hy_f_freq': 1.0 + nrm((n_b, W), 0.05),
        'hy_f_wout': nrm((n_b, W, 2 * HY_ORDER * H), 0.02),
        'hy_f_bias': nrm((n_b, HY_ORDER, H), 0.1),
        'hy_w_out': nrm((n_b, H, D), H ** -0.5),
        'moe_w_router': nrm((DEPTH, D, N_EXPERTS), D ** -0.5),
        'moe_b_router': nrm((DEPTH, N_EXPERTS), 0.01),
        'moe_w1': nrm((DEPTH, N_EXPERTS, D, 2 * F), D ** -0.5),
        'moe_b1': nrm((DEPTH, N_EXPERTS, 2 * F), 0.01),
        'moe_w2': nrm((DEPTH, N_EXPERTS, F, D), F ** -0.5),
        'moe_b2': nrm((DEPTH, N_EXPERTS, D), 0.01),
    }


def reference(x, c, ctx, c_ctx, ada_w, ada_b, norm_g,
              s5_w_in, s5_a_re, s5_a_im, s5_log_dt, s5_b_re, s5_b_im, s5_c_re, s5_c_im, s5_d, s5_w_glu,
              hy_w_in, hy_conv_w, hy_conv_b, hy_f_w1, hy_f_b1, hy_f_w2, hy_f_b2, hy_f_w3, hy_f_b3,
              hy_f_freq, hy_f_wout, hy_f_bias, hy_w_out,
              moe_w_router, moe_b_router, moe_w1, moe_b1, moe_w2, moe_b2):
    s_lat = jax.nn.silu(c)
    s_ctx = jax.nn.silu(c_ctx)
    reads_ctx = [i % N_MIXERS == 0 for i in range(DEPTH)]
    for i in range(DEPTH):
        j = i // N_MIXERS
        update_ctx = any(reads_ctx[i + 1:])
        use_ctx = reads_ctx[i] or update_ctx
        mod = jnp.dot(s_lat, ada_w[i]) + ada_b[i]
        sh1, sc1, g1, sh2, sc2, g2 = [m[:, None, :] for m in jnp.split(mod, 6, axis=-1)]
        h = modulate(rms_norm(x, norm_g[i, 0]), sh1, sc1)
        if use_ctx:
            csh1, csc1, cg1, csh2, csc2, cg2 = jnp.split(jnp.dot(s_ctx, ada_w[i]) + ada_b[i], 6)
            hc = modulate(rms_norm(ctx, norm_g[i, 0]), csh1, csc1)
        if i % N_MIXERS == 0:
            lam_bar, b_bar, c_mat = s5_discretize(s5_a_re[j], s5_a_im[j], s5_log_dt[j],
                                                  s5_b_re[j], s5_b_im[j], s5_c_re[j], s5_c_im[j])
            yc, ctx_states = s5_bidirectional(jnp.dot(hc, s5_w_in[j]), lam_bar, b_bar, c_mat, s5_d[j],
                                              None, update_ctx)
            y, _ = s5_bidirectional(jnp.dot(h, s5_w_in[j]), lam_bar, b_bar, c_mat, s5_d[j],
                                    ctx_states, True)
            y = s5_glu(y, s5_w_glu[j])
            if update_ctx:
                yc = s5_glu(yc, s5_w_glu[j])
        else:
            hy_p = (hy_w_in[j], hy_conv_w[j], hy_conv_b[j], hy_f_w1[j], hy_f_b1[j], hy_f_w2[j], hy_f_b2[j],
                    hy_f_w3[j], hy_f_b3[j], hy_f_freq[j], hy_f_wout[j], hy_f_bias[j], hy_w_out[j])
            y = hyena(h, *hy_p)
            if update_ctx:
                yc = hyena(hc, *hy_p)
        x = x + g1 * rms_norm(y.astype(x.dtype), norm_g[i, 1])
        moe_p = (moe_w_router[i], moe_b_router[i], moe_w1[i], moe_b1[i], moe_w2[i], moe_b2[i])
        hm = modulate(rms_norm(x, norm_g[i, 2]), sh2, sc2)
        x = x + g2 * rms_norm(moe_ffn(hm, *moe_p).astype(x.dtype), norm_g[i, 3])
        if update_ctx:
            ctx = ctx + cg1 * rms_norm(yc.astype(ctx.dtype), norm_g[i, 1])
            hcm = modulate(rms_norm(ctx, norm_g[i, 2]), csh2, csc2)
            ctx = ctx + cg2 * rms_norm(moe_ffn(hcm, *moe_p).astype(ctx.dtype), norm_g[i, 3])
    return x
```

```python
import functools
import math

import numpy as np
import jax
import jax.numpy as jnp
from jax import lax
from jax.experimental import pallas as pl
from jax.experimental.pallas import tpu as pltpu

F32 = jnp.float32
BF16 = jnp.bfloat16
HIGHEST = lax.Precision.HIGHEST

RMS_EPS = 1e-6
MIB = 1024 * 1024

S5_GROUP = 16
S5_STATE = 64
S5_CHUNK = 16
S5_MIN_DT_UNUSED = None
S5_EIG_CLIP = -1e-4
S5_GROUPS_PER_STEP = 8
S5_PITCH_PAD = 8

HY_SHORT = 3
HY_POS_DIM = 33
HY_BANDS = (HY_POS_DIM - 1) // 2
HY_DECAY_TARGET = 1e-2
HY_MAX_DECAY = math.log(HY_DECAY_TARGET) / 0.3
HY_MIN_DECAY = math.log(HY_DECAY_TARGET) / 1.5
DFT_R = 128

TOP_K = 4
SWIGLU_ALPHA = 1.702
SWIGLU_LIMIT = 7.0
MOE_TILE = 256
MOE_ITEM_TILES = 4
MOE_FC = 256
LANES = 128


def _cparams(sem, vmem_mib):
    return pltpu.CompilerParams(dimension_semantics=sem, vmem_limit_bytes=vmem_mib * MIB)


def _norm_mod(x, g, shift, scale):
    ms = jnp.mean(x * x, axis=-1, keepdims=True)
    y = x * lax.rsqrt(ms + RMS_EPS) * g
    return y * (1.0 + scale) + shift


def _adaln_kernel(c_ref, w_ref, b_ref, o_ref):
    c = c_ref[...]
    s = c * jax.nn.sigmoid(c)
    w = w_ref[0]
    r0 = jnp.sum(w * s[:, 0:1], axis=0, keepdims=True)
    r1 = jnp.sum(w * s[:, 1:2], axis=0, keepdims=True)
    o_ref[0] = jnp.concatenate([r0, r1], axis=0) + b_ref[0]


def adaln_mod(c_cols, ada_w, ada_b, tn=1024):
    depth, d, n = ada_w.shape
    return pl.pallas_call(
        _adaln_kernel,
        out_shape=jax.ShapeDtypeStruct((depth, 2, n), F32),
        grid=(depth, n // tn),
        in_specs=[pl.BlockSpec((d, 2), lambda l, j: (0, 0)),
                  pl.BlockSpec((1, d, tn), lambda l, j: (l, 0, j)),
                  pl.BlockSpec((1, 1, tn), lambda l, j: (l, 0, j))],
        out_specs=pl.BlockSpec((1, 2, tn), lambda l, j: (l, 0, j)),
        compiler_params=_cparams(("parallel", "parallel"), 40),
        name="adaln_mod",
    )(c_cols, ada_w, ada_b.reshape(depth, 1, n))


def _nmm_kernel(x_ref, g_ref, sh_ref, sc_ref, w_ref, o_ref):
    h = _norm_mod(x_ref[...], g_ref[...], sh_ref[...], sc_ref[...]).astype(BF16)
    o_ref[...] = jnp.dot(h, w_ref[...], preferred_element_type=F32).astype(o_ref.dtype)


def norm_mod_matmul(x, g, shift, scale, w_bf, out_dtype, tm=256, tn=2048):
    m, d = x.shape
    n = w_bf.shape[1]
    tm = min(tm, m)
    tn = min(tn, n)
    vec = pl.BlockSpec((1, d), lambda j, i: (0, 0))
    return pl.pallas_call(
        _nmm_kernel,
        out_shape=jax.ShapeDtypeStruct((m, n), out_dtype),
        grid=(n // tn, m // tm),
        in_specs=[pl.BlockSpec((tm, d), lambda j, i: (i, 0)), vec, vec, vec,
                  pl.BlockSpec((d, tn), lambda j, i: (0, j))],
        out_specs=pl.BlockSpec((tm, tn), lambda j, i: (i, j)),
        compiler_params=_cparams(("parallel", "parallel"), 48),
        name="norm_mod_matmul",
    )(x, g, shift, scale, w_bf)


def _s5_kernel(ut_ref, uct_ref, mt_ref, bt_ref, ct_ref, lr_ref, li_ref, o_ref,
               sre, sim, hfre, hfim, hbre, hbim, cre, cim, *, n_chunks, n_ctx_chunks, pitch, cpitch):
    gs = S5_GROUPS_PER_STEP
    half = LANES // 2
    for g in range(gs):
        s = jnp.dot(ut_ref[g], bt_ref[g], preferred_element_type=F32)
        sre[g * pitch:g * pitch + n_chunks, :] = s[:, :LANES]
        sim[g * pitch:g * pitch + n_chunks, :] = s[:, LANES:]
        sc = jnp.dot(uct_ref[g], bt_ref[g], preferred_element_type=F32)
        cre[g * cpitch:g * cpitch + n_ctx_chunks, :] = sc[:, :LANES]
        cim[g * cpitch:g * cpitch + n_ctx_chunks, :] = sc[:, LANES:]

    lane = lax.broadcasted_iota(jnp.int32, (gs, LANES), 1)
    fwd = lane < half
    lr = lr_ref[...]
    li = li_ref[...]

    def step(hr, hi, sr, si):
        return lr * hr - li * hi + sr, lr * hi + li * hr + si

    def ctx_body(j, carry):
        hr, hi = carry
        jb = n_ctx_chunks - 1 - j
        sr = jnp.where(fwd, cre[pl.ds(j, gs, stride=cpitch), :], cre[pl.ds(jb, gs, stride=cpitch), :])
        si = jnp.where(fwd, cim[pl.ds(j, gs, stride=cpitch), :], cim[pl.ds(jb, gs, stride=cpitch), :])
        return step(hr, hi, sr, si)

    zero = jnp.zeros((gs, LANES), F32)
    h0 = lax.fori_loop(0, n_ctx_chunks, ctx_body, (zero, zero))

    def body(j, carry):
        hr, hi = carry
        jb = n_chunks - 1 - j
        hfre[pl.ds(j, gs, stride=pitch), :] = hr
        hfim[pl.ds(j, gs, stride=pitch), :] = hi
        hbre[pl.ds(jb, gs, stride=pitch), :] = hr
        hbim[pl.ds(jb, gs, stride=pitch), :] = hi
        sr = jnp.where(fwd, sre[pl.ds(j, gs, stride=pitch), :], sre[pl.ds(jb, gs, stride=pitch), :])
        si = jnp.where(fwd, sim[pl.ds(j, gs, stride=pitch), :], sim[pl.ds(jb, gs, stride=pitch), :])
        return step(hr, hi, sr, si)

    lax.fori_loop(0, n_chunks, body, h0)

    lane_c = lax.broadcasted_iota(jnp.int32, (n_chunks, LANES), 1)
    fwd_c = lane_c < half
    for g in range(gs):
        rows = slice(g * pitch, g * pitch + n_chunks)
        hr = jnp.where(fwd_c, hfre[rows, :], hbre[rows, :])
        hi = jnp.where(fwd_c, hfim[rows, :], hbim[rows, :])
        hin = jnp.concatenate([hr, hi], axis=1).astype(BF16)
        y = jnp.dot(ut_ref[g], mt_ref[g], preferred_element_type=F32)
        y = y + jnp.dot(hin, ct_ref[g], preferred_element_type=F32)
        o_ref[g] = y


def s5_core(ut, uct, mt, bt, ct, lam_r, lam_i):
    groups, n_chunks, kk = ut.shape
    n_ctx_chunks = uct.shape[1]
    gs = S5_GROUPS_PER_STEP
    pitch = n_chunks + S5_PITCH_PAD
    cpitch = n_ctx_chunks + S5_PITCH_PAD
    kern = functools.partial(_s5_kernel, n_chunks=n_chunks, n_ctx_chunks=n_ctx_chunks, pitch=pitch, cpitch=cpitch)
    blk = lambda r: pl.BlockSpec((gs, r, kk), lambda i: (i, 0, 0))
    big = pltpu.VMEM((gs * pitch, LANES), F32)
    small = pltpu.VMEM((gs * cpitch, LANES), F32)
    return pl.pallas_call(
        kern,
        out_shape=jax.ShapeDtypeStruct((groups, n_chunks, kk), F32),
        grid=(groups // gs,),
        in_specs=[blk(n_chunks), blk(n_ctx_chunks), blk(kk), blk(kk), blk(kk),
                  pl.BlockSpec((gs, LANES), lambda i: (i, 0)), pl.BlockSpec((gs, LANES), lambda i: (i, 0))],
        out_specs=blk(n_chunks),
        scratch_shapes=[big, big, big, big, big, big, small, small],
        compiler_params=_cparams(("parallel",), 48),
        name="s5_core",
    )(ut, uct, mt, bt, ct, lam_r, lam_i)


def s5_operators(a_re, a_im, log_dt, b_re, b_im, c_re, c_im, d_skip):
    t = S5_CHUNK
    lam = lax.complex(jnp.minimum(a_re.astype(F32), S5_EIG_CLIP), a_im.astype(F32))
    dt = jnp.exp(log_dt.astype(F32))[..., None]
    lam_dt = lam * dt
    lam_bar = jnp.exp(lam_dt)
    b_bar = ((lam_bar - 1) / lam)[..., None] * lax.complex(b_re.astype(F32), b_im.astype(F32))
    c_mat = lax.complex(c_re.astype(F32), c_im.astype(F32))
    pows = jnp.exp(lam_dt[:, :, None, :] * jnp.arange(t + 1, dtype=F32)[None, None, :, None])
    groups, q = b_bar.shape[1], b_bar.shape[3]

    kern = jnp.einsum('kgcp,kglp,kgpi->kglci', c_mat, pows[:, :, :t], b_bar, precision=HIGHEST).real
    lag = jnp.arange(t)[:, None] - jnp.arange(t)[None, :]
    kf = jnp.where((lag >= 0)[None, :, :, None, None], kern[0][:, jnp.abs(lag)], 0.0)
    kb = jnp.where((lag <= 0)[None, :, :, None, None], kern[1][:, jnp.abs(lag)], 0.0)
    m = kf + kb
    eye_t = jnp.eye(t, dtype=F32)[None, :, :, None, None]
    eye_q = jnp.eye(q, dtype=F32)[None, None, None, :, :]
    m = m + eye_t * eye_q * d_skip.astype(F32).reshape(groups, 1, 1, q, 1)
    mt = m.transpose(0, 2, 4, 1, 3).reshape(groups, t * q, t * q)

    pf = pows[0][:, :t][:, ::-1]
    pb = pows[1][:, :t]
    inj_f = pf[:, :, None, :] * b_bar[0].transpose(0, 2, 1)[:, None, :, :]
    inj_b = pb[:, :, None, :] * b_bar[1].transpose(0, 2, 1)[:, None, :, :]
    bt = jnp.concatenate([inj_f.real, inj_b.real, inj_f.imag, inj_b.imag], axis=-1)
    bt = bt.reshape(groups, t * q, 4 * S5_STATE)

    rf = c_mat[0][:, None, :, :] * pows[0][:, 1:t + 1][:, :, None, :]
    rb = c_mat[1][:, None, :, :] * pows[1][:, 1:t + 1][:, ::-1][:, :, None, :]
    ctm = jnp.concatenate([rf.real, rb.real, -rf.imag, -rb.imag], axis=-1)
    ct = ctm.reshape(groups, t * q, 4 * S5_STATE).transpose(0, 2, 1)

    lam_t = pows[:, :, t]
    lam_r = jnp.concatenate([lam_t[0].real, lam_t[1].real], axis=-1)
    lam_i = jnp.concatenate([lam_t[0].imag, lam_t[1].imag], axis=-1)
    return mt, bt, ct, lam_r, lam_i


def _mm_resid_kernel(y_ref, w_ref, x_ref, gate_ref, gain_ref, o_ref, *, glu):
    y = y_ref[...]
    if glu:
        z = (0.5 * y * (1.0 + lax.erf(y * (1.0 / math.sqrt(2.0))))).astype(BF16)
        ag = jnp.dot(z, w_ref[...], preferred_element_type=F32)
        n = ag.shape[1] // 2
        r = ag[:, :n] * jax.nn.sigmoid(ag[:, n:])
    else:
        r = jnp.dot(y.astype(BF16), w_ref[...], preferred_element_type=F32)
    ms = jnp.mean(r * r, axis=-1, keepdims=True)
    o_ref[...] = x_ref[...] + gate_ref[...] * (r * lax.rsqrt(ms + RMS_EPS) * gain_ref[...])


def matmul_resid(y, w_bf, x, gate, gain, glu, tm=256):
    m, d = y.shape
    n = w_bf.shape[1]
    dout = x.shape[1]
    vec = pl.BlockSpec((1, dout), lambda i: (0, 0))
    return pl.pallas_call(
        functools.partial(_mm_resid_kernel, glu=glu),
        out_shape=jax.ShapeDtypeStruct((m, dout), F32),
        grid=(m // tm,),
        in_specs=[pl.BlockSpec((tm, d), lambda i: (i, 0)),
                  pl.BlockSpec((d, n), lambda i: (0, 0)),
                  pl.BlockSpec((tm, dout), lambda i: (i, 0)), vec, vec],
        out_specs=pl.BlockSpec((tm, dout), lambda i: (i, 0)),
        compiler_params=_cparams(("parallel",), 56),
        name="matmul_resid_glu" if glu else "matmul_resid",
    )(y, w_bf, x, gate, gain)


def _router_kernel(x_ref, g_ref, sh_ref, sc_ref, wr_ref, br_ref, tri_ref, h_ref, ri_ref, rg_ref, cnt_ref, carry):
    i = pl.program_id(0)

    @pl.when(i == 0)
    def _():
        carry[...] = jnp.zeros_like(carry)

    h = _norm_mod(x_ref[...], g_ref[...], sh_ref[...], sc_ref[...])
    h_ref[...] = h
    logits = jnp.dot(h, wr_ref[...], preferred_element_type=F32, precision=HIGHEST) + br_ref[...]
    tm = logits.shape[0]
    lane = lax.broadcasted_iota(jnp.int32, (tm, LANES), 1)
    vals = logits
    top_v, top_i, sels = [], [], []
    for _ in range(TOP_K):
        mx = jnp.max(vals, axis=1, keepdims=True)
        idx = jnp.min(jnp.where(vals == mx, lane, LANES), axis=1, keepdims=True)
        sel = lane == idx
        top_v.append(mx)
        top_i.append(idx)
        sels.append(sel)
        vals = jnp.where(sel, -jnp.inf, vals)
    es = [jnp.exp(v - top_v[0]) for v in top_v]
    den = es[0] + es[1] + es[2] + es[3]
    cnt = jnp.zeros((tm, LANES), F32)
    for sel in sels:
        cnt = cnt + sel.astype(F32)
    before = jnp.dot(tri_ref[...], cnt.astype(BF16), preferred_element_type=F32) + carry[...]
    ri = jnp.zeros((tm, LANES), jnp.int32)
    rg = jnp.zeros((tm, LANES), F32)
    for k in range(TOP_K):
        rank = jnp.sum(jnp.where(sels[k], before, 0.0), axis=1, keepdims=True).astype(jnp.int32)
        ri = jnp.where(lane == k, top_i[k], ri)
        ri = jnp.where(lane == TOP_K + k, rank, ri)
        rg = jnp.where(lane == k, es[k] / den, rg)
    ri_ref[...] = ri
    rg_ref[...] = rg
    carry[...] = carry[...] + jnp.sum(cnt, axis=0, keepdims=True)
    cnt_ref[...] = carry[...]


def moe_router(x, g, shift, scale, w_r, b_r, tm=256):
    t, d = x.shape
    ne = w_r.shape[1]
    wr = jnp.pad(w_r.astype(F32), ((0, 0), (0, LANES - ne)))
    br = jnp.concatenate([b_r.astype(F32), jnp.full((LANES - ne,), -1e30, F32)]).reshape(1, LANES)
    tri = (jnp.arange(tm)[:, None] > jnp.arange(tm)[None, :]).astype(BF16)
    vec = pl.BlockSpec((1, d), lambda i: (0, 0))
    return pl.pallas_call(
        _router_kernel,
        out_shape=(jax.ShapeDtypeStruct((t, d), F32), jax.ShapeDtypeStruct((t, LANES), jnp.int32),
                   jax.ShapeDtypeStruct((t, LANES), F32), jax.ShapeDtypeStruct((1, LANES), F32)),
        grid=(t // tm,),
        in_specs=[pl.BlockSpec((tm, d), lambda i: (i, 0)), vec, vec, vec,
                  pl.BlockSpec((d, LANES), lambda i: (0, 0)), pl.BlockSpec((1, LANES), lambda i: (0, 0)),
                  pl.BlockSpec((tm, tm), lambda i: (0, 0))],
        out_specs=(pl.BlockSpec((tm, d), lambda i: (i, 0)), pl.BlockSpec((tm, LANES), lambda i: (i, 0)),
                   pl.BlockSpec((tm, LANES), lambda i: (i, 0)), pl.BlockSpec((1, LANES), lambda i: (0, 0))),
        scratch_shapes=[pltpu.VMEM((1, LANES), F32)],
        compiler_params=_cparams(("arbitrary",), 32),
        name="moe_router",
    )(x, g, shift, scale, wr, br, tri)


def _moe_kernel(item_e, item_row0, item_nt, item_f,
                tok_ref, h_hbm, w1g_ref, w1l_ref, b1g_ref, b1l_ref, w2_ref, b2_ref, out_hbm,
                xf, xb, acc, w1g_bf, w1l_bf, w2_bf, gsem, osem, *, nf):
    w = pl.program_id(0)
    f = pl.program_id(1)
    nt = item_nt[w]
    tile = MOE_TILE

    def row_copy(tok, r):
        return pltpu.make_async_copy(h_hbm.at[pl.ds(tok, 1)], xf.at[pl.ds(r, 1)], gsem)

    def tile_in_wait():
        return pltpu.make_async_copy(h_hbm.at[pl.ds(0, tile)], xf.at[pl.ds(0, tile)], gsem)

    def tile_out_copy(i):
        r0 = pl.multiple_of(i * tile, tile)
        return pltpu.make_async_copy(acc.at[pl.ds(r0, tile)],
                                     out_hbm.at[pl.ds(pl.multiple_of(item_row0[w] + r0, tile), tile)], osem)

    @pl.when((f == 0) & (nt > 0))
    def _gather():
        def issue(r, c):
            row_copy(tok_ref[0, 0, r], r).start()
            return c
        lax.fori_loop(0, nt * tile, issue, 0)

        def wait_cast(i, c):
            tile_in_wait().wait()
            return c
        lax.fori_loop(0, nt, wait_cast, 0)

        def cast(i, c):
            r0 = pl.multiple_of(i * tile, tile)
            xb[pl.ds(r0, tile), :] = xf[pl.ds(r0, tile), :].astype(BF16)
            acc[pl.ds(r0, tile), :] = jnp.broadcast_to(b2_ref[0], (tile, acc.shape[1]))
            return c
        lax.fori_loop(0, nt, cast, 0)

    @pl.when(nt > 0)
    def _compute():
        w1g_bf[...] = w1g_ref[0].astype(BF16)
        w1l_bf[...] = w1l_ref[0].astype(BF16)
        w2_bf[...] = w2_ref[0].astype(BF16)
        b1g = b1g_ref[0]
        b1l = b1l_ref[0]

        def row_tile(i, c):
            r0 = pl.multiple_of(i * tile, tile)
            xs = xb[pl.ds(r0, tile), :]
            hg = jnp.dot(xs, w1g_bf[...], preferred_element_type=F32) + b1g
            hl = jnp.dot(xs, w1l_bf[...], preferred_element_type=F32) + b1l
            hg = jnp.minimum(hg, SWIGLU_LIMIT)
            hl = jnp.clip(hl, -SWIGLU_LIMIT, SWIGLU_LIMIT)
            act = hg * jax.nn.sigmoid(SWIGLU_ALPHA * hg) * (hl + 1.0)
            acc[pl.ds(r0, tile), :] += jnp.dot(act.astype(BF16), w2_bf[...], preferred_element_type=F32)
            return c
        lax.fori_loop(0, nt, row_tile, 0)

    @pl.when((f == nf - 1) & (nt > 0))
    def _writeback():
        def issue(i, c):
            tile_out_copy(i).start()
            return c
        lax.fori_loop(0, nt, issue, 0)

        def wait(i, c):
            tile_out_copy(i).wait()
            return c
        lax.fori_loop(0, nt, wait, 0)

    @pl.when((w == pl.num_programs(0) - 1) & (f == nf - 1))
    def _zero_tail():
        used = item_f[pl.num_programs(0)]
        total = out_hbm.shape[0] // tile
        acc[0:tile, :] = jnp.zeros((tile, acc.shape[1]), F32)

        def tail_copy(i):
            return pltpu.make_async_copy(acc.at[pl.ds(0, tile)],
                                         out_hbm.at[pl.ds(pl.multiple_of(i * tile, tile), tile)], osem)

        def issue(i, c):
            tail_copy(i).start()
            return c
        lax.fori_loop(used, total, issue, 0)

        def wait(i, c):
            tail_copy(i).wait()
            return c
        lax.fori_loop(used, total, wait, 0)


def moe_experts(h, item_e, item_row0, item_nt, item_f, item_tok, w1, b1, w2, b2, rows_total):
    t, d = h.shape
    ne, _, f2 = w1.shape
    fdim = f2 // 2
    fc = MOE_FC
    nf = fdim // fc
    n_items = item_e.shape[0]
    rmax = MOE_ITEM_TILES * MOE_TILE
    b1r = b1.reshape(ne, 1, f2)
    b2r = b2.reshape(ne, 1, d)
    grid_spec = pltpu.PrefetchScalarGridSpec(
        num_scalar_prefetch=4,
        grid=(n_items, nf),
        in_specs=[
            pl.BlockSpec((1, 1, rmax), lambda w, f, ie, ir, it, jf: (w, 0, 0), memory_space=pltpu.SMEM),
            pl.BlockSpec(memory_space=pl.ANY),
            pl.BlockSpec((1, d, fc), lambda w, f, ie, ir, it, jf: (ie[w], 0, jnp.where(jf[w] > 0, f, nf - 1))),
            pl.BlockSpec((1, d, fc), lambda w, f, ie, ir, it, jf: (ie[w], 0, nf + jnp.where(jf[w] > 0, f, nf - 1))),
            pl.BlockSpec((1, 1, fc), lambda w, f, ie, ir, it, jf: (ie[w], 0, jnp.where(jf[w] > 0, f, nf - 1))),
            pl.BlockSpec((1, 1, fc), lambda w, f, ie, ir, it, jf: (ie[w], 0, nf + jnp.where(jf[w] > 0, f, nf - 1))),
            pl.BlockSpec((1, fc, d), lambda w, f, ie, ir, it, jf: (ie[w], jnp.where(jf[w] > 0, f, nf - 1), 0)),
            pl.BlockSpec((1, 1, d), lambda w, f, ie, ir, it, jf: (ie[w], 0, 0)),
        ],
        out_specs=pl.BlockSpec(memory_space=pl.ANY),
        scratch_shapes=[pltpu.VMEM((rmax, d), F32), pltpu.VMEM((rmax, d), BF16), pltpu.VMEM((rmax, d), F32),
                        pltpu.VMEM((d, fc), BF16), pltpu.VMEM((d, fc), BF16), pltpu.VMEM((fc, d), BF16),
                        pltpu.SemaphoreType.DMA, pltpu.SemaphoreType.DMA],
    )
    return pl.pallas_call(
        functools.partial(_moe_kernel, nf=nf),
        out_shape=jax.ShapeDtypeStruct((rows_total, d), F32),
        grid_spec=grid_spec,
        compiler_params=_cparams(("arbitrary", "arbitrary"), 56),
        name="moe_experts",
    )(item_e, item_row0, item_nt, item_f, item_tok, h, w1, w1, b1r, b1r, w2, b2r)


def moe_plan(route_i, counts):
    t = route_i.shape[0]
    ne = 32
    tile = MOE_TILE
    rmax = MOE_ITEM_TILES * tile
    rows_total = (t * TOP_K + ne * (tile - 1)) // tile * tile
    n_items = rows_total // rmax + ne
    cnt = counts[0, :ne].astype(jnp.int32)
    ntile = (cnt + tile - 1) // tile
    padded = ntile * tile
    pstart = jnp.cumsum(padded) - padded
    e_idx = route_i[:, :TOP_K]
    rank = route_i[:, TOP_K:2 * TOP_K]
    pos = pstart[e_idx] + rank
    tok = jnp.broadcast_to(jnp.arange(t, dtype=jnp.int32)[:, None], (t, TOP_K))
    row_tok = jnp.zeros((rows_total + rmax,), jnp.int32).at[pos.reshape(-1)].set(tok.reshape(-1))
    n_it = (ntile + MOE_ITEM_TILES - 1) // MOE_ITEM_TILES
    it_start = jnp.cumsum(n_it) - n_it
    total_items = jnp.sum(n_it)
    w = jnp.arange(n_items, dtype=jnp.int32)
    e_of = jnp.clip(jnp.searchsorted(jnp.cumsum(n_it), w, side='right'), 0, ne - 1).astype(jnp.int32)
    valid = w < total_items
    last_e = jnp.max(jnp.where(n_it > 0, jnp.arange(ne), 0)).astype(jnp.int32)
    e_of = jnp.where(valid, e_of, last_e)
    sub = w - it_start[e_of]
    tiles_left = ntile[e_of] - sub * MOE_ITEM_TILES
    item_nt = jnp.where(valid, jnp.clip(tiles_left, 0, MOE_ITEM_TILES), 0).astype(jnp.int32)
    item_row0 = jnp.where(valid, pstart[e_of] + sub * rmax, 0).astype(jnp.int32)
    item_f = jnp.concatenate([valid.astype(jnp.int32), jnp.sum(ntile, dtype=jnp.int32).reshape(1)])
    item_tok = row_tok[item_row0[:, None] + jnp.arange(rmax, dtype=jnp.int32)[None, :]].reshape(n_items, 1, rmax)
    return pos.astype(jnp.int32), e_of, item_row0, item_nt, item_f, item_tok, rows_total


def _combine_kernel(pos_ref, o_hbm, gates_ref, x_ref, gate_ref, gain_ref, out_ref, buf, sem, *, tm):
    def row_copy(r, k, t):
        return pltpu.make_async_copy(o_hbm.at[pl.ds(r, 1)], buf.at[k, pl.ds(t, 1)], sem)

    def issue(t, c):
        for k in range(TOP_K):
            row_copy(pos_ref[0, 0, t * TOP_K + k], k, t).start()
        return c
    lax.fori_loop(0, tm, issue, 0)
    for k in range(TOP_K):
        pltpu.make_async_copy(o_hbm.at[pl.ds(0, tm)], buf.at[k], sem).wait()
    gates = gates_ref[...]
    y = buf[0] * gates[:, 0:1]
    for k in range(1, TOP_K):
        y = y + buf[k] * gates[:, k:k + 1]
    ms = jnp.mean(y * y, axis=-1, keepdims=True)
    out_ref[...] = x_ref[...] + gate_ref[...] * (y * lax.rsqrt(ms + RMS_EPS) * gain_ref[...])


def moe_combine(pos, out_sorted, gates, x, gate, gain, tm=256):
    t, d = x.shape
    pos3 = pos.reshape(t // tm, 1, tm * TOP_K)
    vec = pl.BlockSpec((1, d), lambda i: (0, 0))
    return pl.pallas_call(
        functools.partial(_combine_kernel, tm=tm),
        out_shape=jax.ShapeDtypeStruct((t, d), F32),
        grid=(t // tm,),
        in_specs=[pl.BlockSpec((1, 1, tm * TOP_K), lambda i: (i, 0, 0), memory_space=pltpu.SMEM),
                  pl.BlockSpec(memory_space=pl.ANY),
                  pl.BlockSpec((tm, LANES), lambda i: (i, 0)),
                  pl.BlockSpec((tm, d), lambda i: (i, 0)), vec, vec],
        out_specs=pl.BlockSpec((tm, d), lambda i: (i, 0)),
        scratch_shapes=[pltpu.VMEM((TOP_K, tm, d), F32), pltpu.SemaphoreType.DMA],
        compiler_params=_cparams(("arbitrary",), 40),
        name="moe_combine",
    )(pos3, out_sorted, gates, x, gate, gain)


def moe_layer(x, g_norm, shift, scale, gate, gain_out, w_r, b_r, w1, b1, w2, b2):
    h, route_i, route_g, counts = moe_router(x, g_norm, shift, scale, w_r, b_r)
    pos, item_e, item_row0, item_nt, item_f, item_tok, rows_total = moe_plan(route_i, counts)
    out_sorted = moe_experts(h, item_e, item_row0, item_nt, item_f, item_tok, w1, b1, w2, b2, rows_total)
    return moe_combine(pos, out_sorted, route_g, x, gate, gain_out)


HALO = 16


def _hy_in_kernel(xm_ref, xp_ref, xn_ref, g_ref, sh_ref, sc_ref, w_ref, cw_ref, cb_ref, o_ref, hbuf, ubuf):
    i = pl.program_id(1)
    last = pl.num_programs(1) - 1
    tm = xm_ref.shape[0]
    g, sh, sc = g_ref[...], sh_ref[...], sc_ref[...]
    hbuf[0:HALO, :] = _norm_mod(xp_ref[...], g, sh, sc).astype(BF16)
    hbuf[HALO:HALO + tm, :] = _norm_mod(xm_ref[...], g, sh, sc).astype(BF16)
    hbuf[HALO + tm:, :] = _norm_mod(xn_ref[...], g, sh, sc).astype(BF16)
    ubuf[...] = jnp.dot(hbuf[...], w_ref[...], preferred_element_type=F32)

    @pl.when(i == 0)
    def _():
        ubuf[0:HALO, :] = jnp.zeros((HALO, ubuf.shape[1]), F32)

    @pl.when(i == last)
    def _():
        ubuf[HALO + tm:, :] = jnp.zeros((HALO, ubuf.shape[1]), F32)

    cw = cw_ref[...]
    o_ref[...] = (cb_ref[...] + ubuf[pl.ds(HALO - 1, tm), :] * cw[0:1] + ubuf[pl.ds(HALO, tm), :] * cw[1:2]
                  + ubuf[pl.ds(HALO + 1, tm), :] * cw[2:3])


def hyena_inproj(x, g, shift, scale, w_bf, conv_w, conv_b, tm=256, tn=2048):
    m, d = x.shape
    n = w_bf.shape[1]
    hb = tm // HALO
    nh = m // HALO
    vec = pl.BlockSpec((1, d), lambda j, i: (0, 0))
    return pl.pallas_call(
        _hy_in_kernel,
        out_shape=jax.ShapeDtypeStruct((m, n), F32),
        grid=(n // tn, m // tm),
        in_specs=[pl.BlockSpec((tm, d), lambda j, i: (i, 0)),
                  pl.BlockSpec((HALO, d), lambda j, i: (jnp.maximum(i * hb - 1, 0), 0)),
                  pl.BlockSpec((HALO, d), lambda j, i: (jnp.minimum((i + 1) * hb, nh - 1), 0)),
                  vec, vec, vec,
                  pl.BlockSpec((d, tn), lambda j, i: (0, j)),
                  pl.BlockSpec((HY_SHORT, tn), lambda j, i: (0, j)),
                  pl.BlockSpec((1, tn), lambda j, i: (0, j))],
        out_specs=pl.BlockSpec((tm, tn), lambda j, i: (i, j)),
        scratch_shapes=[pltpu.VMEM((tm + 2 * HALO, d), BF16), pltpu.VMEM((tm + 2 * HALO, tn), F32)],
        compiler_params=_cparams(("parallel", "arbitrary"), 48),
        name="hyena_inproj",
    )(x, x, x, g, shift, scale, w_bf, conv_w, conv_b.reshape(1, n))


def dft_tables(r):
    n = r * r
    ar = jnp.arange(r, dtype=jnp.int32)

    def cs(idx, period):
        ang = (2.0 * math.pi / period) * (idx % period).astype(F32)
        return jnp.cos(ang), jnp.sin(ang)

    c, s = cs(ar[None, :, None] * (r * ar[None, None, :] + ar[:, None, None]), n)
    g1 = jnp.concatenate([c, -s], axis=1)
    c, s = cs(ar[:, None] * ar[None, :], r)
    w2 = jnp.concatenate([jnp.concatenate([c, s], axis=1), jnp.concatenate([-s, c], axis=1)], axis=0)
    c, s = cs(ar[None, :, None] * (r * ar[None, None, :] + ar[:, None, None]), n)
    w2i = jnp.concatenate([jnp.concatenate([c, -s], axis=2), jnp.concatenate([s, c], axis=2)], axis=1)
    c, s = cs(ar[:r // 2, None] * ar[None, :], r)
    v3 = jnp.concatenate([c, -s], axis=1) / n
    return g1, w2, w2i, v3


def _hy_filter_s1_kernel(g1_ref, bands_ref, w1_ref, b1_ref, w2_ref, b2_ref, w3_ref, b3_ref, fq_ref, wo0_ref, wo1_ref,
                         dec_ref, o_ref, *, seq):
    r = DFT_R
    n2 = pl.program_id(0)
    row = lax.broadcasted_iota(jnp.int32, (r, LANES), 0)
    lane = lax.broadcasted_iota(jnp.int32, (r, LANES), 1)
    d = row * r + n2
    lag = jnp.where(d < seq, d, 2 * seq - d)
    lagf = lag.astype(F32)
    t = lagf * (1.0 / (seq - 1))
    w = lagf * (2.0 * math.pi / seq)
    phase = bands_ref[...] * w
    feats = jnp.where(lane == 0, t, jnp.where(lane <= HY_BANDS, jnp.cos(phase),
                                              jnp.where(lane <= 2 * HY_BANDS, -jnp.sin(phase), 0.0)))
    fq = fq_ref[...]
    a = jnp.sin(fq * (jnp.dot(feats, w1_ref[...], preferred_element_type=F32, precision=HIGHEST) + b1_ref[...]))
    a = jnp.sin(fq * (jnp.dot(a, w2_ref[...], preferred_element_type=F32, precision=HIGHEST) + b2_ref[...]))
    a = jnp.sin(fq * (jnp.dot(a, w3_ref[...], preferred_element_type=F32, precision=HIGHEST) + b3_ref[...]))
    half = r // 2
    k_pos = jnp.dot(a[:half], wo0_ref[...], preferred_element_type=F32, precision=HIGHEST)
    k_neg = jnp.dot(a[half:], wo1_ref[...], preferred_element_type=F32, precision=HIGHEST)
    k = jnp.concatenate([k_pos, k_neg], axis=0)
    k = k * jnp.exp(-t[:, 0:1] * dec_ref[...])
    k = jnp.where(d[:, 0:1] == seq, 0.0, k)
    b = jnp.dot(g1_ref[0], k.astype(BF16), preferred_element_type=F32)
    o_ref[0] = b[:r].astype(BF16)
    o_ref[1] = b[r:].astype(BF16)


def hyena_filter_s1(g1_bf, f_w1, f_b1, f_w2, f_b2, f_w3, f_b3, f_freq, f_wout, seq, h):
    r = DFT_R
    wd = f_w2.shape[0]
    bands = jnp.linspace(1e-4, HY_BANDS - 1, HY_BANDS, dtype=F32)
    bands_l = jnp.zeros((1, LANES), F32).at[0, 1:1 + HY_BANDS].set(bands).at[0, 1 + HY_BANDS:1 + 2 * HY_BANDS].set(bands)
    w1p = jnp.zeros((LANES, wd), F32).at[:HY_POS_DIM].set(f_w1.astype(F32))
    decay = jnp.abs(jnp.linspace(HY_MIN_DECAY, HY_MAX_DECAY, h, dtype=F32)).reshape(1, h)
    small = lambda shp: pl.BlockSpec(shp, lambda n2, o: (0,) * len(shp))
    row = lambda v: v.astype(F32).reshape(1, -1)
    return pl.pallas_call(
        functools.partial(_hy_filter_s1_kernel, seq=seq),
        out_shape=jax.ShapeDtypeStruct((2, r, r * 2 * h), BF16),
        grid=(r, 2),
        in_specs=[pl.BlockSpec((1, 2 * r, r), lambda n2, o: (n2, 0, 0)),
                  small((1, LANES)), small((LANES, wd)), small((1, wd)), small((wd, wd)), small((1, wd)),
                  small((wd, wd)), small((1, wd)), small((1, wd)),
                  pl.BlockSpec((wd, h), lambda n2, o: (0, o)),
                  pl.BlockSpec((wd, h), lambda n2, o: (0, 2 + o)),
                  small((1, h))],
        out_specs=pl.BlockSpec((2, r, h), lambda n2, o: (0, 0, n2 * 2 + o)),
        compiler_params=_cparams(("parallel", "parallel"), 32),
        name="hyena_filter_s1",
    )(g1_bf, bands_l, w1p, row(f_b1), f_w2.astype(F32), row(f_b2), f_w3.astype(F32), row(f_b3), row(f_freq),
      f_wout.astype(F32), f_wout.astype(F32), decay)


def _hy_filter_s2_kernel(w2_ref, b_ref, o_ref):
    r = DFT_R
    blk = b_ref[...].reshape(2 * r, b_ref.shape[-1])
    x = jnp.dot(w2_ref[...], blk, preferred_element_type=F32)
    o_ref[...] = x.astype(BF16).reshape(o_ref.shape)


def hyena_filter_s2(w2_bf, bk, cb=2048):
    r = DFT_R
    c = bk.shape[-1]
    return pl.pallas_call(
        _hy_filter_s2_kernel,
        out_shape=jax.ShapeDtypeStruct(bk.shape, BF16),
        grid=(r, c // cb),
        in_specs=[pl.BlockSpec((2 * r, 2 * r), lambda k1, j: (0, 0)),
                  pl.BlockSpec((2, 1, r, cb), lambda k1, j: (0, k1, 0, j))],
        out_specs=pl.BlockSpec((2, 1, r, cb), lambda k1, j: (0, k1, 0, j)),
        compiler_params=_cparams(("parallel", "parallel"), 32),
        name="hyena_filter_s2",
    )(w2_bf, bk)


def _hy_conv_s1_kernel(g1_ref, z_ref, o_ref):
    r = DFT_R
    b = jnp.dot(g1_ref[0], z_ref[...].astype(BF16), preferred_element_type=F32)
    o_ref[0] = b[:r].astype(BF16)
    o_ref[1] = b[r:].astype(BF16)


def hyena_conv_s1(g1h_bf, zsrc, h, nblk, off):
    r = DFT_R
    return pl.pallas_call(
        _hy_conv_s1_kernel,
        out_shape=jax.ShapeDtypeStruct((2, r, r * h), BF16),
        grid=(r,),
        in_specs=[pl.BlockSpec((1, 2 * r, r // 2), lambda n2: (n2, 0, 0)),
                  pl.BlockSpec((r // 2, h), lambda n2: (0, n2 * nblk + off))],
        out_specs=pl.BlockSpec((2, r, h), lambda n2: (0, 0, n2)),
        compiler_params=_cparams(("parallel",), 32),
        name="hyena_conv_s1",
    )(g1h_bf, zsrc)


def _hy_conv_s2_kernel(w2_ref, w2i_ref, b_ref, k_ref, o_ref):
    r = DFT_R
    hh = b_ref.shape[-1]
    x = jnp.dot(w2_ref[...], b_ref[...].reshape(2 * r, hh), preferred_element_type=F32)
    xr, xi = x[:r], x[r:]
    kr = k_ref[0, 0].astype(F32)
    ki = k_ref[1, 0].astype(F32)
    y = jnp.concatenate([xr * kr - xi * ki, xr * ki + xi * kr], axis=0).astype(BF16)
    o_ref[...] = jnp.dot(w2i_ref[0], y, preferred_element_type=F32).astype(BF16).reshape(o_ref.shape)


def hyena_conv_s2(w2_bf, w2i_bf, b, kf, order, h):
    r = DFT_R
    return pl.pallas_call(
        _hy_conv_s2_kernel,
        out_shape=jax.ShapeDtypeStruct((2, r, r, h), BF16),
        grid=(r,),
        in_specs=[pl.BlockSpec((2 * r, 2 * r), lambda k1: (0, 0)),
                  pl.BlockSpec((1, 2 * r, 2 * r), lambda k1: (k1, 0, 0)),
                  pl.BlockSpec((2, 1, r, h), lambda k1: (0, k1, 0, 0)),
                  pl.BlockSpec((2, 1, r, h), lambda k1: (0, k1, 0, order))],
        out_specs=pl.BlockSpec((2, 1, r, h), lambda k1: (0, k1, 0, 0)),
        compiler_params=_cparams(("parallel",), 32),
        name="hyena_conv_s2",
    )(w2_bf, w2i_bf, b, kf)


def _hy_conv_s3_kernel(v3_ref, b_ref, gate_ref, z_ref, bias_ref, o_ref):
    r = DFT_R
    conv = jnp.dot(v3_ref[...], b_ref[...].reshape(2 * r, b_ref.shape[-1]), preferred_element_type=F32)
    o_ref[...] = gate_ref[...] * (conv + z_ref[...] * bias_ref[...])


def hyena_conv_s3(v3_bf, b2, gate_src, gate_nblk, gate_off, zsrc, z_nblk, z_off, bias, h):
    r = DFT_R
    return pl.pallas_call(
        _hy_conv_s3_kernel,
        out_shape=jax.ShapeDtypeStruct((r // 2, r * h), F32),
        grid=(r,),
        in_specs=[pl.BlockSpec((r // 2, 2 * r), lambda n2: (0, 0)),
                  pl.BlockSpec((2, r, h), lambda n2: (0, 0, n2)),
                  pl.BlockSpec((r // 2, h), lambda n2: (0, n2 * gate_nblk + gate_off)),
                  pl.BlockSpec((r // 2, h), lambda n2: (0, n2 * z_nblk + z_off)),
                  pl.BlockSpec((1, h), lambda n2: (0, 0))],
        out_specs=pl.BlockSpec((r // 2, h), lambda n2: (0, n2)),
        compiler_params=_cparams(("parallel",), 32),
        name="hyena_conv_s3",
    )(v3_bf, b2, gate_src, zsrc, bias)


def hyena_mixer(x, g, shift, scale, w_in, conv_w, conv_b, f_w1, f_b1, f_w2, f_b2, f_w3, f_b3, f_freq, f_wout, f_bias):
    seq, d = x.shape
    h = w_in.shape[1] // 3
    r = DFT_R
    assert 2 * seq == r * r
    g1, w2, w2i, v3 = dft_tables(r)
    g1_bf = g1.astype(BF16)
    g1h_bf = g1[:, :, :r // 2].astype(BF16)
    w2_bf, w2i_bf, v3_bf = w2.astype(BF16), w2i.astype(BF16), v3.astype(BF16)

    u3 = hyena_inproj(x, g, shift, scale, w_in.astype(BF16), conv_w.astype(F32), conv_b.astype(F32))
    bk = hyena_filter_s1(g1_bf, f_w1, f_b1, f_w2, f_b2, f_w3, f_b3, f_freq, f_wout, seq, h)
    kf = hyena_filter_s2(w2_bf, bk.reshape(2, r, r, 2 * h))

    u3v = u3.reshape(r // 2, r * 3 * h)
    b = hyena_conv_s1(g1h_bf, u3v, h, 3, 2)
    b2 = hyena_conv_s2(w2_bf, w2i_bf, b.reshape(2, r, r, h), kf, 0, h)
    z1 = hyena_conv_s3(v3_bf, b2.reshape(2, r, r * h), u3v, 3, 0, u3v, 3, 2, f_bias[0].astype(F32).reshape(1, h), h)
    b = hyena_conv_s1(g1h_bf, z1, h, 1, 0)
    b2 = hyena_conv_s2(w2_bf, w2i_bf, b.reshape(2, r, r, h), kf, 1, h)
    z2 = hyena_conv_s3(v3_bf, b2.reshape(2, r, r * h), u3v, 3, 1, z1, 1, 0, f_bias[1].astype(F32).reshape(1, h), h)
    return z2.reshape(seq, h)


def kernel(x, c, ctx, c_ctx, ada_w, ada_b, norm_g, s5_w_in, s5_a_re, s5_a_im, s5_log_dt, s5_b_re, s5_b_im, s5_c_re, s5_c_im, s5_d, s5_w_glu, hy_w_in, hy_conv_w, hy_conv_b, hy_f_w1, hy_f_b1, hy_f_w2, hy_f_b2, hy_f_w3, hy_f_b3, hy_f_freq, hy_f_wout, hy_f_bias, hy_w_out, moe_w_router, moe_b_router, moe_w1, moe_b1, moe_w2, moe_b2):
    bsz, seq, d = x.shape
    assert bsz == 1
    xs = x.reshape(seq, d).astype(F32)
    cs = ctx.reshape(-1, d).astype(F32)
    c_cols = jnp.stack([c.reshape(d), c_ctx.reshape(d)], axis=1).astype(F32)
    mod = adaln_mod(c_cols, ada_w.astype(F32), ada_b.astype(F32))

    def mods(layer, which):
        return [mod[layer, which, k * d:(k + 1) * d].reshape(1, d) for k in range(6)]

    gains = norm_g.astype(F32)

    sh1, sc1, g1, sh2, sc2, g2 = mods(0, 0)
    csh1, csc1 = mods(0, 1)[:2]
    gain = lambda layer, k: gains[layer, k].reshape(1, d)
    w_in_bf = s5_w_in[0].astype(BF16)
    u = norm_mod_matmul(xs, gain(0, 0), sh1, sc1, w_in_bf, BF16)
    uc = norm_mod_matmul(cs, gain(0, 0), csh1, csc1, w_in_bf, BF16)
    mt, bt, ct, lam_r, lam_i = s5_operators(s5_a_re[0], s5_a_im[0], s5_log_dt[0], s5_b_re[0], s5_b_im[0],
                                            s5_c_re[0], s5_c_im[0], s5_d[0])
    groups = d // S5_GROUP

    def to_chunks(a):
        n = a.shape[0]
        a = a.reshape(n // S5_CHUNK, S5_CHUNK, groups, S5_GROUP).transpose(2, 0, 1, 3)
        return a.reshape(groups, n // S5_CHUNK, S5_CHUNK * S5_GROUP)

    yt = s5_core(to_chunks(u), to_chunks(uc), mt.astype(BF16), bt.astype(BF16), ct.astype(BF16), lam_r, lam_i)
    y = yt.reshape(groups, seq // S5_CHUNK, S5_CHUNK, S5_GROUP).transpose(1, 2, 0, 3).reshape(seq, d)
    xs = matmul_resid(y, s5_w_glu[0].astype(BF16), xs, g1, gain(0, 1), glu=True)
    xs = moe_layer(xs, gain(0, 2), sh2, sc2, g2, gain(0, 3), moe_w_router[0], moe_b_router[0],
                   moe_w1[0], moe_b1[0], moe_w2[0], moe_b2[0])

    sh1, sc1, g1, sh2, sc2, g2 = mods(1, 0)
    z = hyena_mixer(xs, gain(1, 0), sh1, sc1, hy_w_in[0], hy_conv_w[0], hy_conv_b[0], hy_f_w1[0], hy_f_b1[0],
                    hy_f_w2[0], hy_f_b2[0], hy_f_w3[0], hy_f_b3[0], hy_f_freq[0], hy_f_wout[0], hy_f_bias[0])
    xs = matmul_resid(z, hy_w_out[0].astype(BF16), xs, g1, gain(1, 1), glu=False)
    xs = moe_layer(xs, gain(1, 2), sh2, sc2, g2, gain(1, 3), moe_w_router[1], moe_b_router[1],
                   moe_w1[1], moe_b1[1], moe_w2[1], moe_b2[1])
    return xs.reshape(bsz, seq, d).astype(x.dtype)
```

```python
import functools
import math

import numpy as np
import jax
import jax.numpy as jnp
from jax import lax
from jax.experimental import pallas as pl
from jax.experimental.pallas import tpu as pltpu

F32 = jnp.float32
BF16 = jnp.bfloat16
HIGHEST = lax.Precision.HIGHEST

RMS_EPS = 1e-6
MIB = 1024 * 1024

S5_GROUP = 16
S5_STATE = 64
S5_CHUNK = 16
S5_MIN_DT_UNUSED = None
S5_EIG_CLIP = -1e-4
S5_GROUPS_PER_STEP = 8
S5_PITCH_PAD = 8

HY_SHORT = 3
HY_POS_DIM = 33
HY_BANDS = (HY_POS_DIM - 1) // 2
HY_DECAY_TARGET = 1e-2
HY_MAX_DECAY = math.log(HY_DECAY_TARGET) / 0.3
HY_MIN_DECAY = math.log(HY_DECAY_TARGET) / 1.5
DFT_R = 128

TOP_K = 4
SWIGLU_ALPHA = 1.702
SWIGLU_LIMIT = 7.0
MOE_TILE = 256
MOE_ITEM_TILES = 4
MOE_FC = 256
LANES = 128


def _cparams(sem, vmem_mib):
    return pltpu.CompilerParams(dimension_semantics=sem, vmem_limit_bytes=vmem_mib * MIB)


def _norm_mod(x, g, shift, scale):
    ms = jnp.mean(x * x, axis=-1, keepdims=True)
    y = x * lax.rsqrt(ms + RMS_EPS) * g
    return y * (1.0 + scale) + shift


def _adaln_kernel(c_ref, w_ref, b_ref, o_ref):
    c = c_ref[...]
    s = c * jax.nn.sigmoid(c)
    w = w_ref[0]
    r0 = jnp.sum(w * s[:, 0:1], axis=0, keepdims=True)
    r1 = jnp.sum(w * s[:, 1:2], axis=0, keepdims=True)
    o_ref[0] = jnp.concatenate([r0, r1], axis=0) + b_ref[0]


def adaln_mod(c_cols, ada_w, ada_b, tn=1024):
    depth, d, n = ada_w.shape
    return pl.pallas_call(
        _adaln_kernel,
        out_shape=jax.ShapeDtypeStruct((depth, 2, n), F32),
        grid=(depth, n // tn),
        in_specs=[pl.BlockSpec((d, 2), lambda l, j: (0, 0)),
                  pl.BlockSpec((1, d, tn), lambda l, j: (l, 0, j)),
                  pl.BlockSpec((1, 1, tn), lambda l, j: (l, 0, j))],
        out_specs=pl.BlockSpec((1, 2, tn), lambda l, j: (l, 0, j)),
        compiler_params=_cparams(("parallel", "parallel"), 40),
        name="adaln_mod",
    )(c_cols, ada_w, ada_b.reshape(depth, 1, n))


def _nmm_kernel(x_ref, g_ref, sh_ref, sc_ref, w_ref, o_ref):
    h = _norm_mod(x_ref[...], g_ref[...], sh_ref[...], sc_ref[...]).astype(BF16)
    o_ref[...] = jnp.dot(h, w_ref[...], preferred_element_type=F32).astype(o_ref.dtype)


def norm_mod_matmul(x, g, shift, scale, w_bf, out_dtype, tm=256, tn=2048):
    m, d = x.shape
    n = w_bf.shape[1]
    tm = min(tm, m)
    tn = min(tn, n)
    vec = pl.BlockSpec((1, d), lambda j, i: (0, 0))
    return pl.pallas_call(
        _nmm_kernel,
        out_shape=jax.ShapeDtypeStruct((m, n), out_dtype),
        grid=(n // tn, m // tm),
        in_specs=[pl.BlockSpec((tm, d), lambda j, i: (i, 0)), vec, vec, vec,
                  pl.BlockSpec((d, tn), lambda j, i: (0, j))],
        out_specs=pl.BlockSpec((tm, tn), lambda j, i: (i, j)),
        compiler_params=_cparams(("parallel", "parallel"), 48),
        name="norm_mod_matmul",
    )(x, g, shift, scale, w_bf)


def _s5_kernel(ut_ref, uct_ref, mt_ref, bt_ref, ct_ref, lr_ref, li_ref, o_ref,
               sre, sim, hfre, hfim, hbre, hbim, cre, cim, *, n_chunks, n_ctx_chunks, pitch, cpitch):
    gs = S5_GROUPS_PER_STEP
    half = LANES // 2
    for g in range(gs):
        s = jnp.dot(ut_ref[g], bt_ref[g], preferred_element_type=F32)
        sre[g * pitch:g * pitch + n_chunks, :] = s[:, :LANES]
        sim[g * pitch:g * pitch + n_chunks, :] = s[:, LANES:]
        sc = jnp.dot(uct_ref[g], bt_ref[g], preferred_element_type=F32)
        cre[g * cpitch:g * cpitch + n_ctx_chunks, :] = sc[:, :LANES]
        cim[g * cpitch:g * cpitch + n_ctx_chunks, :] = sc[:, LANES:]

    lane = lax.broadcasted_iota(jnp.int32, (gs, LANES), 1)
    fwd = lane < half
    lr = lr_ref[...]
    li = li_ref[...]

    def step(hr, hi, sr, si):
        return lr * hr - li * hi + sr, lr * hi + li * hr + si

    def ctx_body(j, carry):
        hr, hi = carry
        jb = n_ctx_chunks - 1 - j
        sr = jnp.where(fwd, cre[pl.ds(j, gs, stride=cpitch), :], cre[pl.ds(jb, gs, stride=cpitch), :])
        si = jnp.where(fwd, cim[pl.ds(j, gs, stride=cpitch), :], cim[pl.ds(jb, gs, stride=cpitch), :])
        return step(hr, hi, sr, si)

    zero = jnp.zeros((gs, LANES), F32)
    h0 = lax.fori_loop(0, n_ctx_chunks, ctx_body, (zero, zero))

    def body(j, carry):
        hr, hi = carry
        jb = n_chunks - 1 - j
        hfre[pl.ds(j, gs, stride=pitch), :] = hr
        hfim[pl.ds(j, gs, stride=pitch), :] = hi
        hbre[pl.ds(jb, gs, stride=pitch), :] = hr
        hbim[pl.ds(jb, gs, stride=pitch), :] = hi
        sr = jnp.where(fwd, sre[pl.ds(j, gs, stride=pitch), :], sre[pl.ds(jb, gs, stride=pitch), :])
        si = jnp.where(fwd, sim[pl.ds(j, gs, stride=pitch), :], sim[pl.ds(jb, gs, stride=pitch), :])
        return step(hr, hi, sr, si)

    lax.fori_loop(0, n_chunks, body, h0)

    lane_c = lax.broadcasted_iota(jnp.int32, (n_chunks, LANES), 1)
    fwd_c = lane_c < half
    for g in range(gs):
        rows = slice(g * pitch, g * pitch + n_chunks)
        hr = jnp.where(fwd_c, hfre[rows, :], hbre[rows, :])
        hi = jnp.where(fwd_c, hfim[rows, :], hbim[rows, :])
        hin = jnp.concatenate([hr, hi], axis=1).astype(BF16)
        y = jnp.dot(ut_ref[g], mt_ref[g], preferred_element_type=F32)
        y = y + jnp.dot(hin, ct_ref[g], preferred_element_type=F32)
        o_ref[g] = y


def s5_core(ut, uct, mt, bt, ct, lam_r, lam_i):
    groups, n_chunks, kk = ut.shape
    n_ctx_chunks = uct.shape[1]
    gs = S5_GROUPS_PER_STEP
    pitch = n_chunks + S5_PITCH_PAD
    cpitch = n_ctx_chunks + S5_PITCH_PAD
    kern = functools.partial(_s5_kernel, n_chunks=n_chunks, n_ctx_chunks=n_ctx_chunks, pitch=pitch, cpitch=cpitch)
    blk = lambda r: pl.BlockSpec((gs, r, kk), lambda i: (i, 0, 0))
    big = pltpu.VMEM((gs * pitch, LANES), F32)
    small = pltpu.VMEM((gs * cpitch, LANES), F32)
    return pl.pallas_call(
        kern,
        out_shape=jax.ShapeDtypeStruct((groups, n_chunks, kk), F32),
        grid=(groups // gs,),
        in_specs=[blk(n_chunks), blk(n_ctx_chunks), blk(kk), blk(kk), blk(kk),
                  pl.BlockSpec((gs, LANES), lambda i: (i, 0)), pl.BlockSpec((gs, LANES), lambda i: (i, 0))],
        out_specs=blk(n_chunks),
        scratch_shapes=[big, big, big, big, big, big, small, small],
        compiler_params=_cparams(("parallel",), 48),
        name="s5_core",
    )(ut, uct, mt, bt, ct, lam_r, lam_i)


def s5_operators(a_re, a_im, log_dt, b_re, b_im, c_re, c_im, d_skip):
    t = S5_CHUNK
    lr_ = jnp.minimum(a_re.astype(F32), S5_EIG_CLIP)
    li_ = a_im.astype(F32)
    dt = jnp.exp(log_dt.astype(F32))[..., None]
    e = jnp.arange(t + 1, dtype=F32)[None, None, :, None]
    mag = jnp.exp(lr_[:, :, None, :] * dt[:, :, None, :] * e)
    ang = li_[:, :, None, :] * dt[:, :, None, :] * e
    pw_r, pw_i = mag * jnp.cos(ang), mag * jnp.sin(ang)
    nr, ni = pw_r[:, :, 1] - 1.0, pw_i[:, :, 1]
    den = lr_ * lr_ + li_ * li_
    cf_r = (nr * lr_ + ni * li_) / den
    cf_i = (ni * lr_ - nr * li_) / den
    br_, bi_ = b_re.astype(F32), b_im.astype(F32)
    bb_r = cf_r[..., None] * br_ - cf_i[..., None] * bi_
    bb_i = cf_r[..., None] * bi_ + cf_i[..., None] * br_
    cm_r, cm_i = c_re.astype(F32), c_im.astype(F32)
    groups, q = bb_r.shape[1], bb_r.shape[3]

    class _C:
        def __init__(self, r, i):
            self.real, self.imag = r, i

        def __mul__(self, o):
            return _C(self.real * o.real - self.imag * o.imag, self.real * o.imag + self.imag * o.real)

        def __getitem__(self, idx):
            return _C(self.real[idx], self.imag[idx])

        def transpose(self, *ax):
            return _C(self.real.transpose(*ax), self.imag.transpose(*ax))

    pows = _C(pw_r, pw_i)
    b_bar = _C(bb_r, bb_i)
    c_mat = _C(cm_r, cm_i)

    pb_ = pows[:, :, :t][:, :, :, :, None] * b_bar[:, :, None, :, :]
    kern = (jnp.einsum('kgcp,kglpi->kglci', cm_r, pb_.real, precision=HIGHEST)
            - jnp.einsum('kgcp,kglpi->kglci', cm_i, pb_.imag, precision=HIGHEST))
    lag = jnp.arange(t)[:, None] - jnp.arange(t)[None, :]
    kf = jnp.where((lag >= 0)[None, :, :, None, None], kern[0][:, jnp.abs(lag)], 0.0)
    kb = jnp.where((lag <= 0)[None, :, :, None, None], kern[1][:, jnp.abs(lag)], 0.0)
    m = kf + kb
    eye_t = jnp.eye(t, dtype=F32)[None, :, :, None, None]
    eye_q = jnp.eye(q, dtype=F32)[None, None, None, :, :]
    m = m + eye_t * eye_q * d_skip.astype(F32).reshape(groups, 1, 1, q, 1)
    mt = m.transpose(0, 2, 4, 1, 3).reshape(groups, t * q, t * q)

    pf = pows[0][:, :t][:, ::-1]
    pb = pows[1][:, :t]
    inj_f = pf[:, :, None, :] * b_bar[0].transpose(0, 2, 1)[:, None, :, :]
    inj_b = pb[:, :, None, :] * b_bar[1].transpose(0, 2, 1)[:, None, :, :]
    bt = jnp.concatenate([inj_f.real, inj_b.real, inj_f.imag, inj_b.imag], axis=-1)
    bt = bt.reshape(groups, t * q, 4 * S5_STATE)

    rf = c_mat[0][:, None, :, :] * pows[0][:, 1:t + 1][:, :, None, :]
    rb = c_mat[1][:, None, :, :] * pows[1][:, 1:t + 1][:, ::-1][:, :, None, :]
    ctm = jnp.concatenate([rf.real, rb.real, -rf.imag, -rb.imag], axis=-1)
    ct = ctm.reshape(groups, t * q, 4 * S5_STATE).transpose(0, 2, 1)

    lam_t = pows[:, :, t]
    lam_r = jnp.concatenate([lam_t[0].real, lam_t[1].real], axis=-1)
    lam_i = jnp.concatenate([lam_t[0].imag, lam_t[1].imag], axis=-1)
    return mt, bt, ct, lam_r, lam_i


def _mm_resid_kernel(y_ref, w_ref, x_ref, gate_ref, gain_ref, o_ref, *, glu):
    y = y_ref[...]
    if glu:
        z = (0.5 * y * (1.0 + lax.erf(y * (1.0 / math.sqrt(2.0))))).astype(BF16)
        ag = jnp.dot(z, w_ref[...], preferred_element_type=F32)
        n = ag.shape[1] // 2
        r = ag[:, :n] * jax.nn.sigmoid(ag[:, n:])
    else:
        r = jnp.dot(y.astype(BF16), w_ref[...], preferred_element_type=F32)
    ms = jnp.mean(r * r, axis=-1, keepdims=True)
    o_ref[...] = x_ref[...] + gate_ref[...] * (r * lax.rsqrt(ms + RMS_EPS) * gain_ref[...])


def matmul_resid(y, w_bf, x, gate, gain, glu, tm=256):
    m, d = y.shape
    n = w_bf.shape[1]
    dout = x.shape[1]
    vec = pl.BlockSpec((1, dout), lambda i: (0, 0))
    return pl.pallas_call(
        functools.partial(_mm_resid_kernel, glu=glu),
        out_shape=jax.ShapeDtypeStruct((m, dout), F32),
        grid=(m // tm,),
        in_specs=[pl.BlockSpec((tm, d), lambda i: (i, 0)),
                  pl.BlockSpec((d, n), lambda i: (0, 0)),
                  pl.BlockSpec((tm, dout), lambda i: (i, 0)), vec, vec],
        out_specs=pl.BlockSpec((tm, dout), lambda i: (i, 0)),
        compiler_params=_cparams(("parallel",), 56),
        name="matmul_resid_glu" if glu else "matmul_resid",
    )(y, w_bf, x, gate, gain)


def _router_kernel(x_ref, g_ref, sh_ref, sc_ref, wr_ref, br_ref, tri_ref, h_ref, ri_ref, rg_ref, cnt_ref, carry):
    i = pl.program_id(0)

    @pl.when(i == 0)
    def _():
        carry[...] = jnp.zeros_like(carry)

    h = _norm_mod(x_ref[...], g_ref[...], sh_ref[...], sc_ref[...])
    h_ref[...] = h
    logits = jnp.dot(h, wr_ref[...], preferred_element_type=F32, precision=HIGHEST) + br_ref[...]
    tm = logits.shape[0]
    lane = lax.broadcasted_iota(jnp.int32, (tm, LANES), 1)
    vals = logits
    top_v, top_i, sels = [], [], []
    for _ in range(TOP_K):
        mx = jnp.max(vals, axis=1, keepdims=True)
        idx = jnp.min(jnp.where(vals == mx, lane, LANES), axis=1, keepdims=True)
        sel = lane == idx
        top_v.append(mx)
        top_i.append(idx)
        sels.append(sel)
        vals = jnp.where(sel, -jnp.inf, vals)
    es = [jnp.exp(v - top_v[0]) for v in top_v]
    den = es[0] + es[1] + es[2] + es[3]
    cnt = jnp.zeros((tm, LANES), F32)
    for sel in sels:
        cnt = cnt + sel.astype(F32)
    before = jnp.dot(tri_ref[...], cnt.astype(BF16), preferred_element_type=F32) + carry[...]
    ri = jnp.zeros((tm, LANES), jnp.int32)
    rg = jnp.zeros((tm, LANES), F32)
    for k in range(TOP_K):
        rank = jnp.sum(jnp.where(sels[k], before, 0.0), axis=1, keepdims=True).astype(jnp.int32)
        ri = jnp.where(lane == k, top_i[k], ri)
        ri = jnp.where(lane == TOP_K + k, rank, ri)
        rg = jnp.where(lane == k, es[k] / den, rg)
    ri_ref[...] = ri
    rg_ref[...] = rg
    carry[...] = carry[...] + jnp.sum(cnt, axis=0, keepdims=True)
    cnt_ref[...] = carry[...]


def moe_router(x, g, shift, scale, w_r, b_r, tm=256):
    t, d = x.shape
    ne = w_r.shape[1]
    wr = jnp.pad(w_r.astype(F32), ((0, 0), (0, LANES - ne)))
    br = jnp.concatenate([b_r.astype(F32), jnp.full((LANES - ne,), -1e30, F32)]).reshape(1, LANES)
    tri = (jnp.arange(tm)[:, None] > jnp.arange(tm)[None, :]).astype(BF16)
    vec = pl.BlockSpec((1, d), lambda i: (0, 0))
    return pl.pallas_call(
        _router_kernel,
        out_shape=(jax.ShapeDtypeStruct((t, d), F32), jax.ShapeDtypeStruct((t, LANES), jnp.int32),
                   jax.ShapeDtypeStruct((t, LANES), F32), jax.ShapeDtypeStruct((1, LANES), F32)),
        grid=(t // tm,),
        in_specs=[pl.BlockSpec((tm, d), lambda i: (i, 0)), vec, vec, vec,
                  pl.BlockSpec((d, LANES), lambda i: (0, 0)), pl.BlockSpec((1, LANES), lambda i: (0, 0)),
                  pl.BlockSpec((tm, tm), lambda i: (0, 0))],
        out_specs=(pl.BlockSpec((tm, d), lambda i: (i, 0)), pl.BlockSpec((tm, LANES), lambda i: (i, 0)),
                   pl.BlockSpec((tm, LANES), lambda i: (i, 0)), pl.BlockSpec((1, LANES), lambda i: (0, 0))),
        scratch_shapes=[pltpu.VMEM((1, LANES), F32)],
        compiler_params=_cparams(("arbitrary",), 32),
        name="moe_router",
    )(x, g, shift, scale, wr, br, tri)


def _moe_kernel(item_e, item_row0, item_nt, item_f,
                tok_ref, h_hbm, w1g_ref, w1l_ref, b1g_ref, b1l_ref, w2_ref, b2_ref, out_hbm,
                xf, xb, acc, w1g_bf, w1l_bf, w2_bf, gsem, osem, *, nf):
    w = pl.program_id(0)
    f = pl.program_id(1)
    nt = item_nt[w]
    tile = MOE_TILE

    def row_copy(tok, r):
        return pltpu.make_async_copy(h_hbm.at[pl.ds(tok, 1)], xf.at[pl.ds(r, 1)], gsem)

    def tile_in_wait():
        return pltpu.make_async_copy(h_hbm.at[pl.ds(0, tile)], xf.at[pl.ds(0, tile)], gsem)

    def tile_out_copy(i):
        r0 = pl.multiple_of(i * tile, tile)
        return pltpu.make_async_copy(acc.at[pl.ds(r0, tile)],
                                     out_hbm.at[pl.ds(pl.multiple_of(item_row0[w] + r0, tile), tile)], osem)

    @pl.when((f == 0) & (nt > 0))
    def _gather():
        def issue(r, c):
            row_copy(tok_ref[0, 0, r], r).start()
            return c
        lax.fori_loop(0, nt * tile, issue, 0)

        def wait_cast(i, c):
            tile_in_wait().wait()
            return c
        lax.fori_loop(0, nt, wait_cast, 0)

        def cast(i, c):
            r0 = pl.multiple_of(i * tile, tile)
            xb[pl.ds(r0, tile), :] = xf[pl.ds(r0, tile), :].astype(BF16)
            acc[pl.ds(r0, tile), :] = jnp.broadcast_to(b2_ref[0], (tile, acc.shape[1]))
            return c
        lax.fori_loop(0, nt, cast, 0)

    @pl.when(nt > 0)
    def _compute():
        w1g_bf[...] = w1g_ref[0].astype(BF16)
        w1l_bf[...] = w1l_ref[0].astype(BF16)
        w2_bf[...] = w2_ref[0].astype(BF16)
        b1g = b1g_ref[0]
        b1l = b1l_ref[0]

        def row_tile(i, c):
            r0 = pl.multiple_of(i * tile, tile)
            xs = xb[pl.ds(r0, tile), :]
            hg = jnp.dot(xs, w1g_bf[...], preferred_element_type=F32) + b1g
            hl = jnp.dot(xs, w1l_bf[...], preferred_element_type=F32) + b1l
            hg = jnp.minimum(hg, SWIGLU_LIMIT)
            hl = jnp.clip(hl, -SWIGLU_LIMIT, SWIGLU_LIMIT)
            act = hg * jax.nn.sigmoid(SWIGLU_ALPHA * hg) * (hl + 1.0)
            acc[pl.ds(r0, tile), :] += jnp.dot(act.astype(BF16), w2_bf[...], preferred_element_type=F32)
            return c
        lax.fori_loop(0, nt, row_tile, 0)

    @pl.when((f == nf - 1) & (nt > 0))
    def _writeback():
        def issue(i, c):
            tile_out_copy(i).start()
            return c
        lax.fori_loop(0, nt, issue, 0)

        def wait(i, c):
            tile_out_copy(i).wait()
            return c
        lax.fori_loop(0, nt, wait, 0)

    @pl.when((w == pl.num_programs(0) - 1) & (f == nf - 1))
    def _zero_tail():
        used = item_f[pl.num_programs(0)]
        total = out_hbm.shape[0] // tile
        acc[0:tile, :] = jnp.zeros((tile, acc.shape[1]), F32)

        def tail_copy(i):
            return pltpu.make_async_copy(acc.at[pl.ds(0, tile)],
                                         out_hbm.at[pl.ds(pl.multiple_of(i * tile, tile), tile)], osem)

        def issue(i, c):
            tail_copy(i).start()
            return c
        lax.fori_loop(used, total, issue, 0)

        def wait(i, c):
            tail_copy(i).wait()
            return c
        lax.fori_loop(used, total, wait, 0)


def moe_experts(h, item_e, item_row0, item_nt, item_f, item_tok, w1, b1, w2, b2, layer, rows_total):
    t, d = h.shape
    _, ne, _, f2 = w1.shape
    fdim = f2 // 2
    fc = MOE_FC
    nf = fdim // fc
    n_items = item_e.shape[0]
    rmax = MOE_ITEM_TILES * MOE_TILE
    b1r = b1[layer].reshape(ne, 1, f2)
    b2r = b2[layer].reshape(ne, 1, d)

    def fsel(f, jf, w):
        return jnp.where(jf[w] > 0, f, nf - 1)

    grid_spec = pltpu.PrefetchScalarGridSpec(
        num_scalar_prefetch=4,
        grid=(n_items, nf),
        in_specs=[
            pl.BlockSpec((1, 1, rmax), lambda w, f, ie, ir, it, jf: (w, 0, 0), memory_space=pltpu.SMEM),
            pl.BlockSpec(memory_space=pl.ANY),
            pl.BlockSpec((None, 1, d, fc), lambda w, f, ie, ir, it, jf: (layer, ie[w], 0, fsel(f, jf, w))),
            pl.BlockSpec((None, 1, d, fc), lambda w, f, ie, ir, it, jf: (layer, ie[w], 0, nf + fsel(f, jf, w))),
            pl.BlockSpec((1, 1, fc), lambda w, f, ie, ir, it, jf: (ie[w], 0, fsel(f, jf, w))),
            pl.BlockSpec((1, 1, fc), lambda w, f, ie, ir, it, jf: (ie[w], 0, nf + fsel(f, jf, w))),
            pl.BlockSpec((None, 1, fc, d), lambda w, f, ie, ir, it, jf: (layer, ie[w], fsel(f, jf, w), 0)),
            pl.BlockSpec((1, 1, d), lambda w, f, ie, ir, it, jf: (ie[w], 0, 0)),
        ],
        out_specs=pl.BlockSpec(memory_space=pl.ANY),
        scratch_shapes=[pltpu.VMEM((rmax, d), F32), pltpu.VMEM((rmax, d), BF16), pltpu.VMEM((rmax, d), F32),
                        pltpu.VMEM((d, fc), BF16), pltpu.VMEM((d, fc), BF16), pltpu.VMEM((fc, d), BF16),
                        pltpu.SemaphoreType.DMA, pltpu.SemaphoreType.DMA],
    )
    return pl.pallas_call(
        functools.partial(_moe_kernel, nf=nf),
        out_shape=jax.ShapeDtypeStruct((rows_total, d), F32),
        grid_spec=grid_spec,
        compiler_params=_cparams(("arbitrary", "arbitrary"), 56),
        name="moe_experts",
    )(item_e, item_row0, item_nt, item_f, item_tok, h, w1, w1, b1r, b1r, w2, b2r)


def moe_plan(route_i, counts):
    t = route_i.shape[0]
    ne = 32
    tile = MOE_TILE
    rmax = MOE_ITEM_TILES * tile
    rows_total = (t * TOP_K + ne * (tile - 1)) // tile * tile
    n_items = rows_total // rmax + ne
    cnt = counts[0, :ne].astype(jnp.int32)
    ntile = (cnt + tile - 1) // tile
    padded = ntile * tile
    pstart = jnp.cumsum(padded) - padded
    e_idx = route_i[:, :TOP_K]
    rank = route_i[:, TOP_K:2 * TOP_K]
    pos = pstart[e_idx] + rank
    tok = jnp.broadcast_to(jnp.arange(t, dtype=jnp.int32)[:, None], (t, TOP_K))
    row_tok = jnp.zeros((rows_total + rmax,), jnp.int32).at[pos.reshape(-1)].set(tok.reshape(-1))
    n_it = (ntile + MOE_ITEM_TILES - 1) // MOE_ITEM_TILES
    it_start = jnp.cumsum(n_it) - n_it
    total_items = jnp.sum(n_it)
    w = jnp.arange(n_items, dtype=jnp.int32)
    e_of = jnp.clip(jnp.searchsorted(jnp.cumsum(n_it), w, side='right'), 0, ne - 1).astype(jnp.int32)
    valid = w < total_items
    last_e = jnp.max(jnp.where(n_it > 0, jnp.arange(ne), 0)).astype(jnp.int32)
    e_of = jnp.where(valid, e_of, last_e)
    sub = w - it_start[e_of]
    tiles_left = ntile[e_of] - sub * MOE_ITEM_TILES
    item_nt = jnp.where(valid, jnp.clip(tiles_left, 0, MOE_ITEM_TILES), 0).astype(jnp.int32)
    item_row0 = jnp.where(valid, pstart[e_of] + sub * rmax, 0).astype(jnp.int32)
    item_f = jnp.concatenate([valid.astype(jnp.int32), jnp.sum(ntile, dtype=jnp.int32).reshape(1)])
    item_tok = row_tok[item_row0[:, None] + jnp.arange(rmax, dtype=jnp.int32)[None, :]].reshape(n_items, 1, rmax)
    return pos.astype(jnp.int32), e_of, item_row0, item_nt, item_f, item_tok, rows_total


def _combine_kernel(pos_ref, o_hbm, gates_ref, x_ref, gate_ref, gain_ref, out_ref, buf, sem, *, tm):
    def row_copy(r, k, t):
        return pltpu.make_async_copy(o_hbm.at[pl.ds(r, 1)], buf.at[k, pl.ds(t, 1)], sem)

    def issue(t, c):
        for k in range(TOP_K):
            row_copy(pos_ref[0, 0, t * TOP_K + k], k, t).start()
        return c
    lax.fori_loop(0, tm, issue, 0)
    for k in range(TOP_K):
        pltpu.make_async_copy(o_hbm.at[pl.ds(0, tm)], buf.at[k], sem).wait()
    gates = gates_ref[...]
    y = buf[0] * gates[:, 0:1]
    for k in range(1, TOP_K):
        y = y + buf[k] * gates[:, k:k + 1]
    ms = jnp.mean(y * y, axis=-1, keepdims=True)
    out_ref[...] = x_ref[...] + gate_ref[...] * (y * lax.rsqrt(ms + RMS_EPS) * gain_ref[...])


def moe_combine(pos, out_sorted, gates, x, gate, gain, tm=256):
    t, d = x.shape
    pos3 = pos.reshape(t // tm, 1, tm * TOP_K)
    vec = pl.BlockSpec((1, d), lambda i: (0, 0))
    return pl.pallas_call(
        functools.partial(_combine_kernel, tm=tm),
        out_shape=jax.ShapeDtypeStruct((t, d), F32),
        grid=(t // tm,),
        in_specs=[pl.BlockSpec((1, 1, tm * TOP_K), lambda i: (i, 0, 0), memory_space=pltpu.SMEM),
                  pl.BlockSpec(memory_space=pl.ANY),
                  pl.BlockSpec((tm, LANES), lambda i: (i, 0)),
                  pl.BlockSpec((tm, d), lambda i: (i, 0)), vec, vec],
        out_specs=pl.BlockSpec((tm, d), lambda i: (i, 0)),
        scratch_shapes=[pltpu.VMEM((TOP_K, tm, d), F32), pltpu.SemaphoreType.DMA],
        compiler_params=_cparams(("arbitrary",), 40),
        name="moe_combine",
    )(pos3, out_sorted, gates, x, gate, gain)


def moe_layer(x, g_norm, shift, scale, gate, gain_out, w_r, b_r, w1, b1, w2, b2, layer):
    h, route_i, route_g, counts = moe_router(x, g_norm, shift, scale, w_r, b_r)
    pos, item_e, item_row0, item_nt, item_f, item_tok, rows_total = moe_plan(route_i, counts)
    out_sorted = moe_experts(h, item_e, item_row0, item_nt, item_f, item_tok, w1, b1, w2, b2, layer, rows_total)
    return moe_combine(pos, out_sorted, route_g, x, gate, gain_out)


HALO = 16


def _hy_in_kernel(xm_ref, xp_ref, xn_ref, g_ref, sh_ref, sc_ref, w_ref, cw_ref, cb_ref, o_ref, hbuf, ubuf):
    i = pl.program_id(1)
    last = pl.num_programs(1) - 1
    tm = xm_ref.shape[0]
    g, sh, sc = g_ref[...], sh_ref[...], sc_ref[...]
    hbuf[0:HALO, :] = _norm_mod(xp_ref[...], g, sh, sc).astype(BF16)
    hbuf[HALO:HALO + tm, :] = _norm_mod(xm_ref[...], g, sh, sc).astype(BF16)
    hbuf[HALO + tm:, :] = _norm_mod(xn_ref[...], g, sh, sc).astype(BF16)
    ubuf[...] = jnp.dot(hbuf[...], w_ref[...], preferred_element_type=F32)

    @pl.when(i == 0)
    def _():
        ubuf[0:HALO, :] = jnp.zeros((HALO, ubuf.shape[1]), F32)

    @pl.when(i == last)
    def _():
        ubuf[HALO + tm:, :] = jnp.zeros((HALO, ubuf.shape[1]), F32)

    cw = cw_ref[...]
    o_ref[...] = (cb_ref[...] + ubuf[pl.ds(HALO - 1, tm), :] * cw[0:1] + ubuf[pl.ds(HALO, tm), :] * cw[1:2]
                  + ubuf[pl.ds(HALO + 1, tm), :] * cw[2:3])


def hyena_inproj(x, g, shift, scale, w_bf, conv_w, conv_b, tm=256, tn=2048):
    m, d = x.shape
    n = w_bf.shape[1]
    hb = tm // HALO
    nh = m // HALO
    vec = pl.BlockSpec((1, d), lambda j, i: (0, 0))
    return pl.pallas_call(
        _hy_in_kernel,
        out_shape=jax.ShapeDtypeStruct((m, n), F32),
        grid=(n // tn, m // tm),
        in_specs=[pl.BlockSpec((tm, d), lambda j, i: (i, 0)),
                  pl.BlockSpec((HALO, d), lambda j, i: (jnp.maximum(i * hb - 1, 0), 0)),
                  pl.BlockSpec((HALO, d), lambda j, i: (jnp.minimum((i + 1) * hb, nh - 1), 0)),
                  vec, vec, vec,
                  pl.BlockSpec((d, tn), lambda j, i: (0, j)),
                  pl.BlockSpec((HY_SHORT, tn), lambda j, i: (0, j)),
                  pl.BlockSpec((1, tn), lambda j, i: (0, j))],
        out_specs=pl.BlockSpec((tm, tn), lambda j, i: (i, j)),
        scratch_shapes=[pltpu.VMEM((tm + 2 * HALO, d), BF16), pltpu.VMEM((tm + 2 * HALO, tn), F32)],
        compiler_params=_cparams(("parallel", "arbitrary"), 48),
        name="hyena_inproj",
    )(x, x, x, g, shift, scale, w_bf, conv_w, conv_b.reshape(1, n))


DFT_BLK = 8
DFT_KH = DFT_R // 2 + DFT_BLK


def dft_tables(r):
    n = r * r
    kh = DFT_KH
    ar = jnp.arange(r, dtype=jnp.int32)
    ak = ar[:kh]

    def cs(idx, period):
        ang = (2.0 * math.pi / period) * (idx % period).astype(F32)
        return jnp.cos(ang), jnp.sin(ang)

    c, s = cs(ak[None, :, None] * (r * ar[None, None, :] + ar[:, None, None]), n)
    g1 = jnp.concatenate([c, -s], axis=1)
    c, s = cs(ar[:, None] * ar[None, :], r)
    w2 = jnp.concatenate([jnp.concatenate([c, s], axis=1), jnp.concatenate([-s, c], axis=1)], axis=0)
    c, s = cs(ar[None, :, None] * (r * ar[None, None, :] + ak[:, None, None]), n)
    w2i = jnp.concatenate([jnp.concatenate([c, -s], axis=2), jnp.concatenate([s, c], axis=2)], axis=1)
    c, s = cs(ar[:r // 2, None] * ak[None, :], r)
    wk = jnp.where((ak == 0) | (ak == r // 2), 1.0, jnp.where(ak < r // 2, 2.0, 0.0)).astype(F32)[None, :] / n
    v3 = jnp.concatenate([c * wk, -s * wk], axis=1)
    return g1, w2, w2i, v3


def _hy_filter_s1_kernel(g1_ref, bands_ref, w1_ref, b1_ref, w2_ref, b2_ref, w3_ref, b3_ref, fq_ref, wo0_ref, wo1_ref,
                         dec_ref, o_ref, a_buf, *, seq):
    r = DFT_R
    kh = DFT_KH
    half = r // 2
    nb = pl.program_id(0)
    cc = pl.program_id(1)
    row = lax.broadcasted_iota(jnp.int32, (r, LANES), 0)
    lane = lax.broadcasted_iota(jnp.int32, (r, LANES), 1)

    def lag_of(j):
        d = row * r + (nb * DFT_BLK + j)
        lag = jnp.where(d < seq, d, 2 * seq - d)
        return d, lag.astype(F32)

    @pl.when(cc == 0)
    def _():
        fq = fq_ref[...]
        for j in range(DFT_BLK):
            _, lagf = lag_of(j)
            t = lagf * (1.0 / (seq - 1))
            w = lagf * (2.0 * math.pi / seq)
            phase = bands_ref[...] * w
            feats = jnp.where(lane == 0, t, jnp.where(lane <= HY_BANDS, jnp.cos(phase),
                                                      jnp.where(lane <= 2 * HY_BANDS, -jnp.sin(phase), 0.0)))
            a = jnp.sin(fq * (jnp.dot(feats, w1_ref[...], preferred_element_type=F32, precision=HIGHEST) + b1_ref[...]))
            a = jnp.sin(fq * (jnp.dot(a, w2_ref[...], preferred_element_type=F32, precision=HIGHEST) + b2_ref[...]))
            a = jnp.sin(fq * (jnp.dot(a, w3_ref[...], preferred_element_type=F32, precision=HIGHEST) + b3_ref[...]))
            a_buf[j] = a

    for j in range(DFT_BLK):
        d, lagf = lag_of(j)
        t = lagf[:, 0:1] * (1.0 / (seq - 1))
        a = a_buf[j]
        k_pos = jnp.dot(a[:half], wo0_ref[...], preferred_element_type=F32, precision=HIGHEST)
        k_neg = jnp.dot(a[half:], wo1_ref[...], preferred_element_type=F32, precision=HIGHEST)
        k = jnp.concatenate([k_pos, k_neg], axis=0)
        k = k * jnp.exp(-t * dec_ref[...])
        k = jnp.where(d[:, 0:1] == seq, 0.0, k)
        b = jnp.dot(g1_ref[j], k.astype(BF16), preferred_element_type=F32)
        o_ref[0, :, j, :] = b[:kh]
        o_ref[1, :, j, :] = b[kh:]


def hyena_filter_s1(g1_bf, f_w1, f_b1, f_w2, f_b2, f_w3, f_b3, f_freq, f_wout, seq, h, hc=1024):
    r = DFT_R
    kh = DFT_KH
    wd = f_w2.shape[0]
    ncc = 2 * h // hc
    bands = jnp.linspace(1e-4, HY_BANDS - 1, HY_BANDS, dtype=F32)
    bands_l = jnp.zeros((1, LANES), F32).at[0, 1:1 + HY_BANDS].set(bands).at[0, 1 + HY_BANDS:1 + 2 * HY_BANDS].set(bands)
    w1p = jnp.zeros((LANES, wd), F32).at[:HY_POS_DIM].set(f_w1.astype(F32))
    decay = jnp.abs(jnp.linspace(HY_MIN_DECAY, HY_MAX_DECAY, h, dtype=F32))
    decay2 = jnp.concatenate([decay, decay]).reshape(1, 2 * h)
    small = lambda shp: pl.BlockSpec(shp, lambda nb, c: (0,) * len(shp))
    row = lambda v: v.astype(F32).reshape(1, -1)
    return pl.pallas_call(
        functools.partial(_hy_filter_s1_kernel, seq=seq),
        out_shape=jax.ShapeDtypeStruct((2, kh, r, 2 * h), F32),
        grid=(r // DFT_BLK, ncc),
        in_specs=[pl.BlockSpec((DFT_BLK, 2 * kh, r), lambda nb, c: (nb, 0, 0)),
                  small((1, LANES)), small((LANES, wd)), small((1, wd)), small((wd, wd)), small((1, wd)),
                  small((wd, wd)), small((1, wd)), small((1, wd)),
                  pl.BlockSpec((wd, hc), lambda nb, c: (0, c)),
                  pl.BlockSpec((wd, hc), lambda nb, c: (0, ncc + c)),
                  pl.BlockSpec((1, hc), lambda nb, c: (0, c))],
        out_specs=pl.BlockSpec((2, kh, DFT_BLK, hc), lambda nb, c: (0, 0, nb, c)),
        scratch_shapes=[pltpu.VMEM((DFT_BLK, r, wd), F32)],
        compiler_params=_cparams(("parallel", "arbitrary"), 40),
        name="hyena_filter_s1",
    )(g1_bf, bands_l, w1p, row(f_b1), f_w2.astype(F32), row(f_b2), f_w3.astype(F32), row(f_b3), row(f_freq),
      f_wout.astype(F32), f_wout.astype(F32), decay2)


def _hy_filter_s2_kernel(w2_ref, b_ref, o_ref):
    r = DFT_R
    hc = b_ref.shape[-1]
    for j in range(DFT_BLK):
        blk = b_ref[:, j].reshape(2 * r, hc).astype(BF16)
        x = jnp.dot(w2_ref[...], blk, preferred_element_type=F32)
        o_ref[0, j] = x[:r].astype(BF16)
        o_ref[1, j] = x[r:].astype(BF16)


def hyena_filter_s2(w2_bf, bk, hc=512):
    r = DFT_R
    kh = DFT_KH
    c = bk.shape[-1]
    blk = pl.BlockSpec((2, DFT_BLK, r, hc), lambda kb, j: (0, kb, 0, j))
    return pl.pallas_call(
        _hy_filter_s2_kernel,
        out_shape=jax.ShapeDtypeStruct(bk.shape, BF16),
        grid=(kh // DFT_BLK, c // hc),
        in_specs=[pl.BlockSpec((2 * r, 2 * r), lambda kb, j: (0, 0)), blk],
        out_specs=blk,
        compiler_params=_cparams(("parallel", "parallel"), 32),
        name="hyena_filter_s2",
    )(w2_bf, bk)


def _hy_conv_s1_kernel(g1_ref, z_ref, o_ref):
    kh = DFT_KH
    for j in range(DFT_BLK):
        b = jnp.dot(g1_ref[j], z_ref[:, j, :].astype(BF16), preferred_element_type=F32)
        o_ref[0, :, j, :] = b[:kh]
        o_ref[1, :, j, :] = b[kh:]


def hyena_conv_s1(g1h_bf, zsrc, h, third, hc=1024):
    r = DFT_R
    kh = DFT_KH
    nch = h // hc
    return pl.pallas_call(
        _hy_conv_s1_kernel,
        out_shape=jax.ShapeDtypeStruct((2, kh, r, h), F32),
        grid=(r // DFT_BLK, nch),
        in_specs=[pl.BlockSpec((DFT_BLK, 2 * kh, r // 2), lambda nb, c: (nb, 0, 0)),
                  pl.BlockSpec((r // 2, DFT_BLK, hc), lambda nb, c: (0, nb, third * nch + c))],
        out_specs=pl.BlockSpec((2, kh, DFT_BLK, hc), lambda nb, c: (0, 0, nb, c)),
        compiler_params=_cparams(("parallel", "parallel"), 32),
        name="hyena_conv_s1",
    )(g1h_bf, zsrc)


def _hy_conv_s2_kernel(w2_ref, w2i_ref, b_ref, k_ref, o_ref):
    r = DFT_R
    hc = b_ref.shape[-1]
    for j in range(DFT_BLK):
        x = jnp.dot(w2_ref[...], b_ref[:, j].reshape(2 * r, hc).astype(BF16), preferred_element_type=F32)
        xr, xi = x[:r], x[r:]
        kr = k_ref[0, j].astype(F32)
        ki = k_ref[1, j].astype(F32)
        y = jnp.concatenate([xr * kr - xi * ki, xr * ki + xi * kr], axis=0).astype(BF16)
        out = jnp.dot(w2i_ref[j], y, preferred_element_type=F32)
        o_ref[0, :, j, :] = out[:r]
        o_ref[1, :, j, :] = out[r:]


def hyena_conv_s2(w2_bf, w2i_bf, b, kf, order, h, hc=512):
    r = DFT_R
    kh = DFT_KH
    nch = h // hc
    return pl.pallas_call(
        _hy_conv_s2_kernel,
        out_shape=jax.ShapeDtypeStruct((2, r, kh, h), F32),
        grid=(kh // DFT_BLK, nch),
        in_specs=[pl.BlockSpec((2 * r, 2 * r), lambda kb, c: (0, 0)),
                  pl.BlockSpec((DFT_BLK, 2 * r, 2 * r), lambda kb, c: (kb, 0, 0)),
                  pl.BlockSpec((2, DFT_BLK, r, hc), lambda kb, c: (0, kb, 0, c)),
                  pl.BlockSpec((2, DFT_BLK, r, hc), lambda kb, c: (0, kb, 0, order * nch + c))],
        out_specs=pl.BlockSpec((2, r, DFT_BLK, hc), lambda kb, c: (0, 0, kb, c)),
        compiler_params=_cparams(("parallel", "parallel"), 40),
        name="hyena_conv_s2",
    )(w2_bf, w2i_bf, b, kf)


def _hy_conv_s3_kernel(v3_ref, b_ref, gate_ref, z_ref, bias_ref, o_ref):
    kh = DFT_KH
    hc = b_ref.shape[-1]
    bias = bias_ref[...]
    for j in range(DFT_BLK):
        conv = jnp.dot(v3_ref[...], b_ref[:, j].reshape(2 * kh, hc).astype(BF16), preferred_element_type=F32)
        o_ref[:, j, :] = gate_ref[:, j, :] * (conv + z_ref[:, j, :] * bias)


def hyena_conv_s3(v3_bf, b2, gate_src, gate_third, zsrc, z_third, bias, h, hc=1024):
    r = DFT_R
    kh = DFT_KH
    nch = h // hc
    return pl.pallas_call(
        _hy_conv_s3_kernel,
        out_shape=jax.ShapeDtypeStruct((r // 2, r, h), F32),
        grid=(r // DFT_BLK, nch),
        in_specs=[pl.BlockSpec((r // 2, 2 * kh), lambda nb, c: (0, 0)),
                  pl.BlockSpec((2, DFT_BLK, kh, hc), lambda nb, c: (0, nb, 0, c)),
                  pl.BlockSpec((r // 2, DFT_BLK, hc), lambda nb, c: (0, nb, gate_third * nch + c)),
                  pl.BlockSpec((r // 2, DFT_BLK, hc), lambda nb, c: (0, nb, z_third * nch + c)),
                  pl.BlockSpec((1, hc), lambda nb, c: (0, c))],
        out_specs=pl.BlockSpec((r // 2, DFT_BLK, hc), lambda nb, c: (0, nb, c)),
        compiler_params=_cparams(("parallel", "parallel"), 40),
        name="hyena_conv_s3",
    )(v3_bf, b2, gate_src, zsrc, bias)


def hyena_mixer(x, g, shift, scale, w_in, conv_w, conv_b, f_w1, f_b1, f_w2, f_b2, f_w3, f_b3, f_freq, f_wout, f_bias):
    seq, d = x.shape
    h = w_in.shape[1] // 3
    r = DFT_R
    assert 2 * seq == r * r
    g1, w2, w2i, v3 = dft_tables(r)
    g1_bf = g1.astype(BF16)
    g1h_bf = g1[:, :, :r // 2].astype(BF16)
    w2_bf, w2i_bf, v3_bf = w2.astype(BF16), w2i.astype(BF16), v3.astype(BF16)

    u3 = hyena_inproj(x, g, shift, scale, w_in.astype(BF16), conv_w.astype(F32), conv_b.astype(F32))
    bk = hyena_filter_s1(g1_bf, f_w1, f_b1, f_w2, f_b2, f_w3, f_b3, f_freq, f_wout, seq, h)
    kf = hyena_filter_s2(w2_bf, bk)

    u3v = u3.reshape(r // 2, r, 3 * h)
    bias = lambda o: f_bias[o].astype(F32).reshape(1, h)
    b = hyena_conv_s1(g1h_bf, u3v, h, 2)
    b2 = hyena_conv_s2(w2_bf, w2i_bf, b, kf, 0, h)
    z1 = hyena_conv_s3(v3_bf, b2, u3v, 0, u3v, 2, bias(0), h)
    b = hyena_conv_s1(g1h_bf, z1, h, 0)
    b2 = hyena_conv_s2(w2_bf, w2i_bf, b, kf, 1, h)
    z2 = hyena_conv_s3(v3_bf, b2, u3v, 1, z1, 0, bias(1), h)
    return z2.reshape(seq, h)


def kernel(x, c, ctx, c_ctx, ada_w, ada_b, norm_g, s5_w_in, s5_a_re, s5_a_im, s5_log_dt, s5_b_re, s5_b_im, s5_c_re, s5_c_im, s5_d, s5_w_glu, hy_w_in, hy_conv_w, hy_conv_b, hy_f_w1, hy_f_b1, hy_f_w2, hy_f_b2, hy_f_w3, hy_f_b3, hy_f_freq, hy_f_wout, hy_f_bias, hy_w_out, moe_w_router, moe_b_router, moe_w1, moe_b1, moe_w2, moe_b2):
    bsz, seq, d = x.shape
    assert bsz == 1
    xs = x.reshape(seq, d).astype(F32)
    cs = ctx.reshape(-1, d).astype(F32)
    c_cols = jnp.stack([c.reshape(d), c_ctx.reshape(d)], axis=1).astype(F32)
    mod = adaln_mod(c_cols, ada_w.astype(F32), ada_b.astype(F32))

    def mods(layer, which):
        return [mod[layer, which, k * d:(k + 1) * d].reshape(1, d) for k in range(6)]

    gains = norm_g.astype(F32)

    sh1, sc1, g1, sh2, sc2, g2 = mods(0, 0)
    csh1, csc1 = mods(0, 1)[:2]
    gain = lambda layer, k: gains[layer, k].reshape(1, d)
    w_in_bf = s5_w_in[0].astype(BF16)
    u = norm_mod_matmul(xs, gain(0, 0), sh1, sc1, w_in_bf, BF16)
    uc = norm_mod_matmul(cs, gain(0, 0), csh1, csc1, w_in_bf, BF16)
    mt, bt, ct, lam_r, lam_i = s5_operators(s5_a_re[0], s5_a_im[0], s5_log_dt[0], s5_b_re[0], s5_b_im[0],
                                            s5_c_re[0], s5_c_im[0], s5_d[0])
    groups = d // S5_GROUP

    def to_chunks(a):
        n = a.shape[0]
        a = a.reshape(n // S5_CHUNK, S5_CHUNK, groups, S5_GROUP).transpose(2, 0, 1, 3)
        return a.reshape(groups, n // S5_CHUNK, S5_CHUNK * S5_GROUP)

    yt = s5_core(to_chunks(u), to_chunks(uc), mt.astype(BF16), bt.astype(BF16), ct.astype(BF16), lam_r, lam_i)
    y = yt.reshape(groups, seq // S5_CHUNK, S5_CHUNK, S5_GROUP).transpose(1, 2, 0, 3).reshape(seq, d)
    xs = matmul_resid(y, s5_w_glu[0].astype(BF16), xs, g1, gain(0, 1), glu=True)
    xs = moe_layer(xs, gain(0, 2), sh2, sc2, g2, gain(0, 3), moe_w_router[0], moe_b_router[0],
                   moe_w1, moe_b1, moe_w2, moe_b2, 0)

    sh1, sc1, g1, sh2, sc2, g2 = mods(1, 0)
    z = hyena_mixer(xs, gain(1, 0), sh1, sc1, hy_w_in[0], hy_conv_w[0], hy_conv_b[0], hy_f_w1[0], hy_f_b1[0],
                    hy_f_w2[0], hy_f_b2[0], hy_f_w3[0], hy_f_b3[0], hy_f_freq[0], hy_f_wout[0], hy_f_bias[0])
    xs = matmul_resid(z, hy_w_out[0].astype(BF16), xs, g1, gain(1, 1), glu=False)
    xs = moe_layer(xs, gain(1, 2), sh2, sc2, g2, gain(1, 3), moe_w_router[1], moe_b_router[1],
                   moe_w1, moe_b1, moe_w2, moe_b2, 1)
    return xs.reshape(bsz, seq, d).astype(x.dtype)
```
